```python
import math
import jax
import jax.numpy as jnp
from jax import lax
import numpy as np

D_MODEL = 1024
BATCH = 8
SEQ = 4096
DEPTH = 4

CHUNK = 64
Q_BLOCK = 128
HEAD_DIM = 64
FOX_HEADS = D_MODEL // HEAD_DIM
DIFF_HEADS = D_MODEL // (2 * HEAD_DIM)
ROT_DIM = HEAD_DIM // 4
ROPE_THETA = 500000.0
CONV_WIDTH = 31
D_FF = ((8 * D_MODEL // 3 + 127) // 128) * 128
N_MIXERS = 3
N_FOX = (DEPTH + 2) // 3
N_CONV = (DEPTH + 1) // 3
N_DIFF = DEPTH // 3
EPS = 1e-6

kernel_name = 'hybrid_fox_conformer_diffattn_trunk'


def _rms_norm(x, g):
    xf = x.astype(jnp.float32)
    y = xf * lax.rsqrt(jnp.mean(xf * xf, axis=-1, keepdims=True) + EPS)
    return (y * g.astype(jnp.float32)).astype(x.dtype)


def _layer_norm(x, g, b):
    xf = x.astype(jnp.float32)
    mu = jnp.mean(xf, axis=-1, keepdims=True)
    var = jnp.mean(jnp.square(xf - mu), axis=-1, keepdims=True)
    y = (xf - mu) * lax.rsqrt(var + EPS)
    return (y * g.astype(jnp.float32) + b.astype(jnp.float32)).astype(x.dtype)


def _swiglu(h, w_gate, w_up, w_down):
    return (jax.nn.silu(h @ w_gate) * (h @ w_up)) @ w_down


def _partial_rope(x, cos, sin):
    half = ROT_DIM // 2
    x1 = x[..., :half]
    x2 = x[..., half:ROT_DIM]
    xp = x[..., ROT_DIM:]
    return jnp.concatenate([x1 * cos - x2 * sin, x2 * cos + x1 * sin, xp], axis=-1)


def _fox_mixer(h, w_in, b_f, q_gain, k_gain, w_out):
    bsz, seq, _ = h.shape
    proj = h @ w_in
    q = proj[..., :D_MODEL].reshape(bsz, seq, FOX_HEADS, HEAD_DIM)
    k = proj[..., D_MODEL:2 * D_MODEL].reshape(bsz, seq, FOX_HEADS, HEAD_DIM)
    v = proj[..., 2 * D_MODEL:3 * D_MODEL].reshape(bsz, seq, FOX_HEADS, HEAD_DIM)
    f_logit = proj[..., 3 * D_MODEL:] + b_f
    q = _rms_norm(q, q_gain)
    k = _rms_norm(k, k_gain)
    log_f = jax.nn.log_sigmoid(f_logit.astype(jnp.float32))
    cum = jnp.cumsum(log_f, axis=1).transpose(0, 2, 1)
    scale = HEAD_DIM ** -0.5
    outs = []
    for q0 in range(0, seq, Q_BLOCK):
        q1 = q0 + Q_BLOCK
        s = jnp.einsum('bqhd,bkhd->bhqk', q[:, q0:q1], k[:, :q1]).astype(jnp.float32) * scale
        s = s + cum[:, :, q0:q1, None] - cum[:, :, None, :q1]
        causal = jnp.arange(q1)[None, :] <= jnp.arange(q0, q1)[:, None]
        p = jax.nn.softmax(jnp.where(causal, s, -jnp.inf), axis=-1)
        outs.append(jnp.einsum('bhqk,bkhd->bqhd', p.astype(v.dtype), v[:, :q1]))
    o = jnp.concatenate(outs, axis=1).reshape(bsz, seq, D_MODEL)
    return o @ w_out


def _conv_module(h, w_pw1, b_pw1, w_dw, b_dw, ln_g, ln_b, w_pw2, b_pw2):
    u = h @ w_pw1 + b_pw1
    u = u[..., :D_MODEL] * jax.nn.sigmoid(u[..., D_MODEL:])
    u = lax.conv_general_dilated(
        u, w_dw[:, None, :].astype(u.dtype), window_strides=(1,),
        padding=[(CONV_WIDTH - 1, 0)], dimension_numbers=('NWC', 'WIO', 'NWC'),
        feature_group_count=D_MODEL) + b_dw
    u = jax.nn.silu(_layer_norm(u, ln_g, ln_b))
    return u @ w_pw2 + b_pw2


def _diff_mixer(h, cos, sin, w_in, q_gain, k_gain, lam_q1, lam_k1, lam_q2, lam_k2,
                sub_gain, w_out, lambda_init):
    bsz, seq, _ = h.shape
    proj = h @ w_in
    q = proj[..., :D_MODEL].reshape(bsz, seq, DIFF_HEADS, 2, HEAD_DIM)
    k = proj[..., D_MODEL:2 * D_MODEL].reshape(bsz, seq, DIFF_HEADS, 2, HEAD_DIM)
    v = proj[..., 2 * D_MODEL:].reshape(bsz, seq, DIFF_HEADS, 2 * HEAD_DIM)
    q = _partial_rope(_rms_norm(q, q_gain), cos, sin)
    k = _partial_rope(_rms_norm(k, k_gain), cos, sin)
    lam = (jnp.exp(jnp.sum(lam_q1.astype(jnp.float32) * lam_k1.astype(jnp.float32)))
           - jnp.exp(jnp.sum(lam_q2.astype(jnp.float32) * lam_k2.astype(jnp.float32)))
           + lambda_init)
    scale = HEAD_DIM ** -0.5
    outs = []
    for q0 in range(0, seq, Q_BLOCK):
        q1 = q0 + Q_BLOCK
        s = jnp.einsum('bqhmd,bkhmd->bhmqk', q[:, q0:q1], k[:, :q1]).astype(jnp.float32) * scale
        visible = (jnp.arange(q1)[None, :] // CHUNK) <= (jnp.arange(q0, q1)[:, None] // CHUNK)
        p = jax.nn.softmax(jnp.where(visible, s, -jnp.inf), axis=-1)
        a = p[:, :, 0] - lam * p[:, :, 1]
        outs.append(jnp.einsum('bhqk,bkhe->bqhe', a.astype(v.dtype), v[:, :q1]))
    o = jnp.concatenate(outs, axis=1)
    o = _rms_norm(o, sub_gain) * (1.0 - lambda_init)
    return o.reshape(bsz, seq, D_MODEL) @ w_out


def setup_inputs(seed: int = 0) -> dict:
    key = jax.random.key(seed)
    ks = iter(list(jax.random.split(key, 40)))
    f32 = jnp.float32

    def nrm(shape, fan_in):
        return jax.random.normal(next(ks), shape, f32) * (fan_in ** -0.5)

    def gain(shape):
        return 1.0 + 0.05 * jax.random.normal(next(ks), shape, f32)

    def bias(shape):
        return 0.02 * jax.random.normal(next(ks), shape, f32)

    x = jax.random.normal(next(ks), (BATCH, SEQ, D_MODEL), f32)
    offset = CHUNK * jax.random.randint(next(ks), (BATCH, 1), 0, 64, dtype=jnp.int32)
    positions = (jnp.arange(SEQ, dtype=jnp.int32)[None, :] + offset).astype(jnp.int32)
    return {
        'x': x,
        'positions': positions,
        'norm_ffn1': gain((DEPTH, D_MODEL)),
        'norm_mix': gain((DEPTH, D_MODEL)),
        'norm_ffn2': gain((DEPTH, D_MODEL)),
        'ffn1_w_gate': nrm((DEPTH, D_MODEL, D_FF), D_MODEL),
        'ffn1_w_up': nrm((DEPTH, D_MODEL, D_FF), D_MODEL),
        'ffn1_w_down': nrm((DEPTH, D_FF, D_MODEL), D_FF),
        'ffn2_w_gate': nrm((DEPTH, D_MODEL, D_FF), D_MODEL),
        'ffn2_w_up': nrm((DEPTH, D_MODEL, D_FF), D_MODEL),
        'ffn2_w_down': nrm((DEPTH, D_FF, D_MODEL), D_FF),
        'fox_w_in': nrm((N_FOX, D_MODEL, 3 * D_MODEL + FOX_HEADS), D_MODEL),
        'fox_b_f': 1.0 + 3.0 * jax.random.uniform(next(ks), (N_FOX, FOX_HEADS), f32),
        'fox_q_gain': gain((N_FOX, HEAD_DIM)),
        'fox_k_gain': gain((N_FOX, HEAD_DIM)),
        'fox_w_out': nrm((N_FOX, D_MODEL, D_MODEL), D_MODEL),
        'conv_w_pw1': nrm((N_CONV, D_MODEL, 2 * D_MODEL), D_MODEL),
        'conv_b_pw1': bias((N_CONV, 2 * D_MODEL)),
        'conv_w_dw': nrm((N_CONV, CONV_WIDTH, D_MODEL), CONV_WIDTH),
        'conv_b_dw': bias((N_CONV, D_MODEL)),
        'conv_ln_g': gain((N_CONV, D_MODEL)),
        'conv_ln_b': bias((N_CONV, D_MODEL)),
        'conv_w_pw2': nrm((N_CONV, D_MODEL, D_MODEL), D_MODEL),
        'conv_b_pw2': bias((N_CONV, D_MODEL)),
        'diff_w_in': nrm((N_DIFF, D_MODEL, 3 * D_MODEL), D_MODEL),
        'diff_q_gain': gain((N_DIFF, HEAD_DIM)),
        'diff_k_gain': gain((N_DIFF, HEAD_DIM)),
        'diff_lambda_q1': 0.1 * jax.random.normal(next(ks), (N_DIFF, HEAD_DIM), f32),
        'diff_lambda_k1': 0.1 * jax.random.normal(next(ks), (N_DIFF, HEAD_DIM), f32),
        'diff_lambda_q2': 0.1 * jax.random.normal(next(ks), (N_DIFF, HEAD_DIM), f32),
        'diff_lambda_k2': 0.1 * jax.random.normal(next(ks), (N_DIFF, HEAD_DIM), f32),
        'diff_sub_gain': gain((N_DIFF, 2 * HEAD_DIM)),
        'diff_w_out': nrm((N_DIFF, D_MODEL, D_MODEL), D_MODEL),
    }


def reference(x, positions, norm_ffn1, norm_mix, norm_ffn2,
              ffn1_w_gate, ffn1_w_up, ffn1_w_down, ffn2_w_gate, ffn2_w_up, ffn2_w_down,
              fox_w_in, fox_b_f, fox_q_gain, fox_k_gain, fox_w_out,
              conv_w_pw1, conv_b_pw1, conv_w_dw, conv_b_dw, conv_ln_g, conv_ln_b,
              conv_w_pw2, conv_b_pw2,
              diff_w_in, diff_q_gain, diff_k_gain, diff_lambda_q1, diff_lambda_k1,
              diff_lambda_q2, diff_lambda_k2, diff_sub_gain, diff_w_out):
    inv_freq = ROPE_THETA ** (-jnp.arange(0, ROT_DIM, 2, dtype=jnp.float32) / ROT_DIM)
    ang = positions.astype(jnp.float32)[..., None] * inv_freq
    cos = jnp.cos(ang)[:, :, None, None, :].astype(x.dtype)
    sin = jnp.sin(ang)[:, :, None, None, :].astype(x.dtype)

    i_fox = 0
    i_conv = 0
    i_diff = 0
    for i in range(DEPTH):
        x = x + 0.5 * _swiglu(_rms_norm(x, norm_ffn1[i]), ffn1_w_gate[i], ffn1_w_up[i], ffn1_w_down[i])
        h = _rms_norm(x, norm_mix[i])
        kind = i % N_MIXERS
        if kind == 0:
            y = _fox_mixer(h, fox_w_in[i_fox], fox_b_f[i_fox], fox_q_gain[i_fox],
                           fox_k_gain[i_fox], fox_w_out[i_fox])
            i_fox += 1
        elif kind == 1:
            y = _conv_module(h, conv_w_pw1[i_conv], conv_b_pw1[i_conv], conv_w_dw[i_conv],
                             conv_b_dw[i_conv], conv_ln_g[i_conv], conv_ln_b[i_conv],
                             conv_w_pw2[i_conv], conv_b_pw2[i_conv])
            i_conv += 1
        else:
            lambda_init = 0.8 - 0.6 * math.exp(-0.3 * i)
            y = _diff_mixer(h, cos, sin, diff_w_in[i_diff], diff_q_gain[i_diff],
                            diff_k_gain[i_diff], diff_lambda_q1[i_diff], diff_lambda_k1[i_diff],
                            diff_lambda_q2[i_diff], diff_lambda_k2[i_diff],
                            diff_sub_gain[i_diff], diff_w_out[i_diff], lambda_init)
            i_diff += 1
        x = x + y
        x = x + 0.5 * _swiglu(_rms_norm(x, norm_ffn2[i]), ffn2_w_gate[i], ffn2_w_up[i], ffn2_w_down[i])
    return x
```

```python
import functools
import math

import jax
import jax.numpy as jnp
from jax import lax
from jax.experimental import pallas as pl
from jax.experimental.pallas import tpu as pltpu

D_MODEL = 1024
HEAD_DIM = 64
FOX_HEADS = D_MODEL // HEAD_DIM
DIFF_HEADS = D_MODEL // (2 * HEAD_DIM)
ROT_DIM = HEAD_DIM // 4
ROPE_THETA = 500000.0
CONV_WIDTH = 31
CHUNK = 64
N_MIXERS = 3
EPS = 1e-6
SCALE = HEAD_DIM ** -0.5

VMEM_LIMIT_BYTES = 56 * 1024 * 1024
LANES = 128
SLAB = 128
AUG_ROWS = 16

FFN_TM = 512
ATT_T = 512
CONV_TM = 256
CONV_RC = 32
CONV_HALO = 32
FFN_FC = 256

F32 = jnp.float32
BF16 = jnp.bfloat16
NT_DIMS = (((1,), (1,)), ((), ()))


def _rms(x, g):
    return x * lax.rsqrt(jnp.mean(x * x, axis=-1, keepdims=True) + EPS) * g


def _resident(shape):
    nd = len(shape)
    return pl.BlockSpec(shape, lambda *_: (0,) * nd, pipeline_mode=pl.Buffered(1))


def _layer_weight(shape):
    nd = len(shape)

    def make(l):
        return pl.BlockSpec((None,) + tuple(shape), lambda *_: (l,) + (0,) * nd,
                            pipeline_mode=pl.Buffered(1))
    return make


def _params(sem):
    return pltpu.CompilerParams(dimension_semantics=sem, vmem_limit_bytes=VMEM_LIMIT_BYTES)


def _ffn_kernel(*refs, has_pre):
    if has_pre:
        o_ref, wo_ref, x_ref, g_ref, wg_ref, wu_ref, wd_ref, out_ref = refs
        x = x_ref[...] + jnp.dot(o_ref[...], wo_ref[...], preferred_element_type=F32)
    else:
        x_ref, g_ref, wg_ref, wu_ref, wd_ref, out_ref = refs
        x = x_ref[...]
    h = _rms(x, g_ref[...]).astype(BF16)
    d_ff = wg_ref.shape[1]
    y = jnp.zeros_like(x)
    for c in range(d_ff // FFN_FC):
        sl = slice(c * FFN_FC, (c + 1) * FFN_FC)
        gate = jnp.dot(h, wg_ref[:, sl], preferred_element_type=F32)
        up = jnp.dot(h, wu_ref[:, sl], preferred_element_type=F32)
        a = (gate * jax.nn.sigmoid(gate) * up).astype(BF16)
        y = y + jnp.dot(a, wd_ref[sl, :], preferred_element_type=F32)
    out_ref[...] = x + 0.5 * y


def _ffn(x2d, gain, wg, wu, wd, layer, pre=None):
    n, d = x2d.shape
    d_ff = wg.shape[-1]
    row = lambda i: (i, 0)
    in_specs = [pl.BlockSpec((FFN_TM, d), row),
                _layer_weight((1, d))(layer),
                _layer_weight((d, d_ff))(layer),
                _layer_weight((d, d_ff))(layer),
                _layer_weight((d_ff, d))(layer)]
    args = [x2d, gain, wg, wu, wd]
    if pre is not None:
        o2d, wo, wo_layer = pre
        in_specs = [pl.BlockSpec((FFN_TM, d), row), _layer_weight((d, d))(wo_layer)] + in_specs
        args = [o2d, wo] + args
    return pl.pallas_call(
        functools.partial(_ffn_kernel, has_pre=pre is not None),
        out_shape=jax.ShapeDtypeStruct((n, d), F32),
        grid=(n // FFN_TM,),
        in_specs=in_specs,
        out_specs=pl.BlockSpec((FFN_TM, d), row),
        compiler_params=_params(("parallel",)),
        name="ffn_pre" if pre is not None else "ffn",
    )(*args)


def _head_norm_t(xt, gain_col):
    return xt * lax.rsqrt(jnp.mean(xt * xt, axis=0, keepdims=True) + EPS) * gain_col


def _split3(c):
    hi = c.astype(BF16).astype(F32)
    mid = (c - hi).astype(BF16).astype(F32)
    lo = (c - hi - mid).astype(BF16).astype(F32)
    return hi, mid, lo


def _fox_proj_kernel(x_ref, g_ref, wqk_ref, wv_ref, wf_ref, bf_ref, qg_ref, kg_ref, tri_ref,
                     q_out, k_out, v_out, carry_ref):
    tm = x_ref.shape[0]

    @pl.when(pl.program_id(1) == 0)
    def _():
        carry_ref[...] = jnp.zeros_like(carry_ref)

    h = _rms(x_ref[...], g_ref[...]).astype(BF16)
    qk = lax.dot_general(wqk_ref[...], h, NT_DIMS, preferred_element_type=F32)
    v_out[...] = jnp.dot(h, wv_ref[...], preferred_element_type=F32).astype(BF16)
    f = lax.dot_general(wf_ref[...], h, NT_DIMS, preferred_element_type=F32) + bf_ref[...]
    logf = jax.nn.log_sigmoid(f)
    tri = tri_ref[...]
    cum = carry_ref[...]
    for piece in _split3(logf):
        cum = cum + jnp.dot(piece.astype(BF16), tri, preferred_element_type=F32)
    carry_ref[...] = cum[:, tm - 1:tm]

    row = lax.broadcasted_iota(jnp.int32, (AUG_ROWS, tm), 0)
    zeros_tail = jnp.zeros((SLAB - HEAD_DIM - AUG_ROWS, tm), BF16)
    for hh in range(FOX_HEADS):
        c_hi, c_mid, c_lo = _split3(cum[hh:hh + 1, :])
        q_aug = jnp.where(row < 3, 1.0,
                          jnp.where(row == 3, c_hi, jnp.where(row == 4, c_mid,
                                                              jnp.where(row == 5, c_lo, 0.0))))
        k_aug = jnp.where(row == 0, -c_hi,
                          jnp.where(row == 1, -c_mid, jnp.where(row == 2, -c_lo,
                                                                jnp.where(row < 6, 1.0, 0.0))))
        qt = qk[hh * HEAD_DIM:(hh + 1) * HEAD_DIM, :]
        kt = qk[D_MODEL + hh * HEAD_DIM:D_MODEL + (hh + 1) * HEAD_DIM, :]
        q_out[hh, 0:HEAD_DIM, :] = (_head_norm_t(qt, qg_ref[...]) * SCALE).astype(BF16)
        k_out[hh, 0:HEAD_DIM, :] = _head_norm_t(kt, kg_ref[...]).astype(BF16)
        q_out[hh, HEAD_DIM:HEAD_DIM + AUG_ROWS, :] = q_aug.astype(BF16)
        k_out[hh, HEAD_DIM:HEAD_DIM + AUG_ROWS, :] = k_aug.astype(BF16)
        q_out[hh, HEAD_DIM + AUG_ROWS:SLAB, :] = zeros_tail
        k_out[hh, HEAD_DIM + AUG_ROWS:SLAB, :] = zeros_tail


def _fox_proj(x, gain, wqk_t, wv, wf_t, b_f, q_gain, k_gain, tri, layer, idx):
    b, s, d = x.shape
    nt = s // ATT_T
    slab_shape = jax.ShapeDtypeStruct((b, nt, FOX_HEADS, SLAB, ATT_T), BF16)
    slab_spec = pl.BlockSpec((None, None, FOX_HEADS, SLAB, ATT_T), lambda bi, i: (bi, i, 0, 0, 0))
    return pl.pallas_call(
        _fox_proj_kernel,
        out_shape=(slab_shape, slab_shape, jax.ShapeDtypeStruct((b, s, d), BF16)),
        grid=(b, nt),
        in_specs=[pl.BlockSpec((None, ATT_T, d), lambda bi, i: (bi, i, 0)),
                  _layer_weight((1, d))(layer),
                  _layer_weight((2 * d, d))(idx),
                  _layer_weight((d, d))(idx),
                  _layer_weight((FOX_HEADS, d))(idx),
                  _layer_weight((FOX_HEADS, 1))(idx),
                  _layer_weight((HEAD_DIM, 1))(idx),
                  _layer_weight((HEAD_DIM, 1))(idx),
                  _resident((ATT_T, ATT_T))],
        out_specs=(slab_spec, slab_spec, pl.BlockSpec((None, ATT_T, d), lambda bi, i: (bi, i, 0))),
        scratch_shapes=[pltpu.VMEM((FOX_HEADS, 1), F32)],
        compiler_params=_params(("parallel", "arbitrary")),
        name="fox_proj",
    )(x, gain, wqk_t, wv, wf_t, b_f, q_gain, k_gain, tri)


def _attend(qn_ref, m_ref, l_ref, acc_ref, k_of, v_ref, qi, mask):
    t = qn_ref.shape[1]
    m_ref[...] = jnp.full_like(m_ref, -1e30)
    l_ref[...] = jnp.zeros_like(l_ref)
    acc_ref[...] = jnp.zeros_like(acc_ref)

    def block(ki, masked):
        vblk = v_ref[pl.ds(pl.multiple_of(ki * t, t), t), :]
        for j in range(2):
            s = jnp.dot(qn_ref[j], k_of(ki, j), preferred_element_type=F32)
            if masked:
                s = jnp.where(mask, s, -jnp.inf)
            m_prev = m_ref[j]
            m_new = jnp.maximum(m_prev, jnp.max(s, axis=1, keepdims=True))
            alpha = jnp.exp(m_prev - m_new)
            p = jnp.exp(s - m_new)
            l_ref[j] = alpha * l_ref[j] + jnp.sum(p, axis=1, keepdims=True)
            acc_ref[j] = alpha * acc_ref[j] + jnp.dot(p.astype(BF16), vblk,
                                                      preferred_element_type=F32)
            m_ref[j] = m_new

    def body(ki, carry):
        block(ki, False)
        return carry

    lax.fori_loop(0, qi, body, 0)
    block(qi, True)


def _attn_scratch():
    return [pltpu.VMEM((2, ATT_T, SLAB), BF16),
            pltpu.VMEM((2, ATT_T, 1), F32),
            pltpu.VMEM((2, ATT_T, 1), F32),
            pltpu.VMEM((2, ATT_T, LANES), F32)]


def _fox_attn_kernel(q_ref, k_ref, v_ref, o_ref, qn_ref, m_ref, l_ref, acc_ref):
    t = o_ref.shape[0]
    qi = pl.program_id(2)
    for j in range(2):
        qn_ref[j] = q_ref[j].astype(F32).T.astype(BF16)
    row = lax.broadcasted_iota(jnp.int32, (t, t), 0)
    col = lax.broadcasted_iota(jnp.int32, (t, t), 1)
    _attend(qn_ref, m_ref, l_ref, acc_ref, lambda ki, j: k_ref[ki, j], v_ref, qi, col <= row)
    lane = lax.broadcasted_iota(jnp.int32, (t, LANES), 1)
    o = jnp.where(lane < HEAD_DIM, acc_ref[0] / l_ref[0], acc_ref[1] / l_ref[1])
    o_ref[...] = o.astype(BF16)


def _fox_attn(q_t, k_t, v):
    b, s, d = v.shape
    nt = s // ATT_T
    return pl.pallas_call(
        _fox_attn_kernel,
        out_shape=jax.ShapeDtypeStruct((b, s, d), BF16),
        grid=(b, FOX_HEADS // 2, nt),
        in_specs=[pl.BlockSpec((None, None, 2, SLAB, ATT_T), lambda bi, hp, qi: (bi, qi, hp, 0, 0)),
                  pl.BlockSpec((None, nt, 2, SLAB, ATT_T), lambda bi, hp, qi: (bi, 0, hp, 0, 0)),
                  pl.BlockSpec((None, s, LANES), lambda bi, hp, qi: (bi, 0, hp))],
        out_specs=pl.BlockSpec((None, ATT_T, LANES), lambda bi, hp, qi: (bi, qi, hp)),
        scratch_shapes=_attn_scratch(),
        compiler_params=_params(("parallel", "parallel", "arbitrary")),
        name="fox_attn",
    )(q_t, k_t, v)


def _diff_proj_kernel(x_ref, pos_ref, g_ref, wqk_ref, wv_ref, qg_ref, kg_ref, invf_ref,
                      q_out, k_out, v_out):
    half = ROT_DIM // 2
    h = _rms(x_ref[...], g_ref[...]).astype(BF16)
    qk = lax.dot_general(wqk_ref[...], h, NT_DIMS, preferred_element_type=F32)
    v_out[...] = jnp.dot(h, wv_ref[...], preferred_element_type=F32).astype(BF16)
    ang = invf_ref[...] * pos_ref[...].astype(F32)
    cos = jnp.cos(ang)
    sin = jnp.sin(ang)

    def rope_t(xt):
        x1, x2 = xt[0:half], xt[half:ROT_DIM]
        return jnp.concatenate([x1 * cos - x2 * sin, x2 * cos + x1 * sin, xt[ROT_DIM:]], axis=0)

    for gi in range(2 * DIFF_HEADS):
        hh, mp = divmod(gi, 2)
        rows = slice(mp * HEAD_DIM, (mp + 1) * HEAD_DIM)
        qt = qk[gi * HEAD_DIM:(gi + 1) * HEAD_DIM, :]
        kt = qk[D_MODEL + gi * HEAD_DIM:D_MODEL + (gi + 1) * HEAD_DIM, :]
        q_out[hh, rows, :] = (rope_t(_head_norm_t(qt, qg_ref[...])) * SCALE).astype(BF16)
        k_out[hh, rows, :] = rope_t(_head_norm_t(kt, kg_ref[...])).astype(BF16)


def _diff_proj(x, pos, gain, wqk_t, wv, q_gain, k_gain, inv_freq, layer, idx):
    b, s, d = x.shape
    nt = s // ATT_T
    slab_shape = jax.ShapeDtypeStruct((b, nt, DIFF_HEADS, SLAB, ATT_T), BF16)
    slab_spec = pl.BlockSpec((None, None, DIFF_HEADS, SLAB, ATT_T), lambda bi, i: (bi, i, 0, 0, 0))
    return pl.pallas_call(
        _diff_proj_kernel,
        out_shape=(slab_shape, slab_shape, jax.ShapeDtypeStruct((b, s, d), BF16)),
        grid=(b, nt),
        in_specs=[pl.BlockSpec((None, ATT_T, d), lambda bi, i: (bi, i, 0)),
                  pl.BlockSpec((None, 1, ATT_T), lambda bi, i: (bi, 0, i)),
                  _layer_weight((1, d))(layer),
                  _layer_weight((2 * d, d))(idx),
                  _layer_weight((d, d))(idx),
                  _layer_weight((HEAD_DIM, 1))(idx),
                  _layer_weight((HEAD_DIM, 1))(idx),
                  _resident((ROT_DIM // 2, 1))],
        out_specs=(slab_spec, slab_spec, pl.BlockSpec((None, ATT_T, d), lambda bi, i: (bi, i, 0))),
        compiler_params=_params(("parallel", "parallel")),
        name="diff_proj",
    )(x, pos, gain, wqk_t, wv, q_gain, k_gain, inv_freq)


def _diff_attn_kernel(q_ref, k_ref, v_ref, lq1_ref, lk1_ref, lq2_ref, lk2_ref, sg_ref, o_ref,
                      qn_ref, m_ref, l_ref, acc_ref, *, lambda_init):
    t = o_ref.shape[0]
    qi = pl.program_id(2)
    q = q_ref[...].astype(F32).T
    lane = lax.broadcasted_iota(jnp.int32, (t, SLAB), 1)
    qn_ref[0] = jnp.where(lane < HEAD_DIM, q, 0.0).astype(BF16)
    qn_ref[1] = jnp.where(lane >= HEAD_DIM, q, 0.0).astype(BF16)
    row = lax.broadcasted_iota(jnp.int32, (t, t), 0)
    col = lax.broadcasted_iota(jnp.int32, (t, t), 1)
    _attend(qn_ref, m_ref, l_ref, acc_ref, lambda ki, j: k_ref[ki], v_ref, qi,
            (col // CHUNK) <= (row // CHUNK))
    lam = (jnp.exp(jnp.sum(lq1_ref[...] * lk1_ref[...], axis=1, keepdims=True))
           - jnp.exp(jnp.sum(lq2_ref[...] * lk2_ref[...], axis=1, keepdims=True)) + lambda_init)
    o = acc_ref[0] / l_ref[0] - lam * (acc_ref[1] / l_ref[1])
    o_ref[...] = (_rms(o, sg_ref[...]) * (1.0 - lambda_init)).astype(BF16)


def _diff_attn(q_t, k_t, v, lq1, lk1, lq2, lk2, sub_gain, idx, lambda_init):
    b, s, d = v.shape
    nt = s // ATT_T
    vec = _layer_weight((1, HEAD_DIM))(idx)
    return pl.pallas_call(
        functools.partial(_diff_attn_kernel, lambda_init=lambda_init),
        out_shape=jax.ShapeDtypeStruct((b, s, d), BF16),
        grid=(b, DIFF_HEADS, nt),
        in_specs=[pl.BlockSpec((None, None, None, SLAB, ATT_T), lambda bi, hh, qi: (bi, qi, hh, 0, 0)),
                  pl.BlockSpec((None, nt, None, SLAB, ATT_T), lambda bi, hh, qi: (bi, 0, hh, 0, 0)),
                  pl.BlockSpec((None, s, LANES), lambda bi, hh, qi: (bi, 0, hh)),
                  vec, vec, vec, vec,
                  _layer_weight((1, 2 * HEAD_DIM))(idx)],
        out_specs=pl.BlockSpec((None, ATT_T, LANES), lambda bi, hh, qi: (bi, qi, hh)),
        scratch_shapes=_attn_scratch(),
        compiler_params=_params(("parallel", "parallel", "arbitrary")),
        name="diff_attn",
    )(q_t, k_t, v, lq1, lk1, lq2, lk2, sub_gain)


def _conv_kernel(x_ref, g_ref, w1_ref, b1_ref, wdw_ref, bdw_ref, lng_ref, lnb_ref, w2_ref, b2_ref,
                 out_ref, u_ref, a_ref):
    tm, d = x_ref.shape

    @pl.when(pl.program_id(1) == 0)
    def _():
        u_ref[0:CONV_HALO, :] = jnp.zeros((CONV_HALO, d), F32)

    @pl.when(pl.program_id(1) > 0)
    def _():
        u_ref[0:CONV_HALO, :] = u_ref[tm:tm + CONV_HALO, :]

    x = x_ref[...]
    h = _rms(x, g_ref[...]).astype(BF16)
    u = jnp.dot(h, w1_ref[...], preferred_element_type=F32) + b1_ref[...]
    u_ref[CONV_HALO:CONV_HALO + tm, :] = u[:, :d] * jax.nn.sigmoid(u[:, d:])

    first = CONV_HALO - (CONV_WIDTH - 1)
    for c in range(tm // CONV_RC):
        base = c * CONV_RC
        acc = jnp.broadcast_to(bdw_ref[...], (CONV_RC, d))
        for j in range(CONV_WIDTH):
            acc = acc + wdw_ref[j:j + 1, :] * u_ref[base + first + j:base + first + j + CONV_RC, :]
        mu = jnp.mean(acc, axis=-1, keepdims=True)
        cen = acc - mu
        var = jnp.mean(cen * cen, axis=-1, keepdims=True)
        y = cen * lax.rsqrt(var + EPS) * lng_ref[...] + lnb_ref[...]
        a_ref[base:base + CONV_RC, :] = (y * jax.nn.sigmoid(y)).astype(BF16)

    out_ref[...] = x + jnp.dot(a_ref[...], w2_ref[...], preferred_element_type=F32) + b2_ref[...]


def _conv_module(x, gain, w1, b1, wdw, bdw, lng, lnb, w2, b2, layer, idx):
    b, s, d = x.shape
    tile = pl.BlockSpec((None, CONV_TM, d), lambda bi, i: (bi, i, 0))
    vec = _layer_weight((1, d))(idx)
    return pl.pallas_call(
        _conv_kernel,
        out_shape=jax.ShapeDtypeStruct((b, s, d), F32),
        grid=(b, s // CONV_TM),
        in_specs=[tile,
                  _layer_weight((1, d))(layer),
                  _layer_weight((d, 2 * d))(idx),
                  _layer_weight((1, 2 * d))(idx),
                  _layer_weight((CONV_WIDTH, d))(idx),
                  vec, vec, vec,
                  _layer_weight((d, d))(idx),
                  vec],
        out_specs=tile,
        scratch_shapes=[pltpu.VMEM((CONV_HALO + CONV_TM, d), F32),
                        pltpu.VMEM((CONV_TM, d), BF16)],
        compiler_params=_params(("parallel", "arbitrary")),
        name="conv_module",
    )(x, gain, w1, b1, wdw, bdw, lng, lnb, w2, b2)


def kernel(x, positions, norm_ffn1, norm_mix, norm_ffn2, ffn1_w_gate, ffn1_w_up, ffn1_w_down, ffn2_w_gate, ffn2_w_up, ffn2_w_down, fox_w_in, fox_b_f, fox_q_gain, fox_k_gain, fox_w_out, conv_w_pw1, conv_b_pw1, conv_w_dw, conv_b_dw, conv_ln_g, conv_ln_b, conv_w_pw2, conv_b_pw2, diff_w_in, diff_q_gain, diff_k_gain, diff_lambda_q1, diff_lambda_k1, diff_lambda_q2, diff_lambda_k2, diff_sub_gain, diff_w_out):
    b, s, d = x.shape
    depth = norm_ffn1.shape[0]
    assert d == D_MODEL and s % ATT_T == 0 and s % CONV_TM == 0 and (b * s) % FFN_TM == 0

    bf = lambda w: w.astype(BF16)
    row = lambda v: v[:, None, :]
    col = lambda v: v[:, :, None]
    g1, gm, g2 = row(norm_ffn1), row(norm_mix), row(norm_ffn2)
    f1 = (bf(ffn1_w_gate), bf(ffn1_w_up), bf(ffn1_w_down))
    f2 = (bf(ffn2_w_gate), bf(ffn2_w_up), bf(ffn2_w_down))
    fox_wqk_t = bf(jnp.swapaxes(fox_w_in[:, :, :2 * d], 1, 2))
    fox_wv = bf(fox_w_in[:, :, 2 * d:3 * d])
    fox_wf_t = bf(jnp.swapaxes(fox_w_in[:, :, 3 * d:], 1, 2))
    fox_wo = bf(fox_w_out)
    diff_wqk_t = bf(jnp.swapaxes(diff_w_in[:, :, :2 * d], 1, 2))
    diff_wv = bf(diff_w_in[:, :, 2 * d:])
    diff_wo = bf(diff_w_out)
    conv_w1, conv_w2 = bf(conv_w_pw1), bf(conv_w_pw2)
    tri = (jnp.arange(ATT_T)[:, None] <= jnp.arange(ATT_T)[None, :]).astype(BF16)
    inv_freq = (ROPE_THETA ** (-jnp.arange(0, ROT_DIM, 2, dtype=F32) / ROT_DIM))[:, None]
    pos = positions[:, None, :]

    flat = lambda a: a.reshape(b * s, d)
    i_fox = i_conv = i_diff = 0
    x2 = flat(x)
    for i in range(depth):
        x2 = _ffn(x2, g1, *f1, i)
        x3 = x2.reshape(b, s, d)
        kind = i % N_MIXERS
        if kind == 0:
            q_t, k_t, v = _fox_proj(x3, gm, fox_wqk_t, fox_wv, fox_wf_t, col(fox_b_f),
                                    col(fox_q_gain), col(fox_k_gain), tri, i, i_fox)
            o = _fox_attn(q_t, k_t, v)
            x2 = _ffn(x2, g2, *f2, i, pre=(flat(o), fox_wo, i_fox))
            i_fox += 1
        elif kind == 1:
            x3 = _conv_module(x3, gm, conv_w1, row(conv_b_pw1), conv_w_dw, row(conv_b_dw),
                              row(conv_ln_g), row(conv_ln_b), conv_w2, row(conv_b_pw2), i, i_conv)
            x2 = _ffn(flat(x3), g2, *f2, i)
            i_conv += 1
        else:
            lambda_init = 0.8 - 0.6 * math.exp(-0.3 * i)
            q_t, k_t, v = _diff_proj(x3, pos, gm, diff_wqk_t, diff_wv, col(diff_q_gain),
                                     col(diff_k_gain), inv_freq, i, i_diff)
            o = _diff_attn(q_t, k_t, v, row(diff_lambda_q1), row(diff_lambda_k1),
                           row(diff_lambda_q2), row(diff_lambda_k2), row(diff_sub_gain),
                           i_diff, lambda_init)
            x2 = _ffn(x2, g2, *f2, i, pre=(flat(o), diff_wo, i_diff))
            i_diff += 1
    return x2.reshape(b, s, d)
```

```python
import functools
import math

import jax
import jax.numpy as jnp
from jax import lax
from jax.experimental import pallas as pl
from jax.experimental.pallas import tpu as pltpu

D_MODEL = 1024
HEAD_DIM = 64
FOX_HEADS = D_MODEL // HEAD_DIM
DIFF_HEADS = D_MODEL // (2 * HEAD_DIM)
ROT_DIM = HEAD_DIM // 4
ROPE_THETA = 500000.0
CONV_WIDTH = 31
CHUNK = 64
N_MIXERS = 3
EPS = 1e-6
LOG2E = math.log2(math.e)
QSCALE = HEAD_DIM ** -0.5 * LOG2E

VMEM_LIMIT_BYTES = 56 * 1024 * 1024
SLAB = 128
AUG_ROWS = 16

FFN_TM = 512
ATT_T = 512
ATT_TK = 256
CONV_TM = 256
CONV_RC = 32
CONV_HALO = 32
FFN_FC = 256

F32 = jnp.float32
BF16 = jnp.bfloat16
NT_DIMS = (((1,), (1,)), ((), ()))


def _rms(x, g):
    return x * lax.rsqrt(jnp.mean(x * x, axis=-1, keepdims=True) + EPS) * g


def _resident(shape):
    nd = len(shape)
    return pl.BlockSpec(shape, lambda *_: (0,) * nd, pipeline_mode=pl.Buffered(1))


def _layer_weight(shape):
    nd = len(shape)

    def make(l):
        return pl.BlockSpec((None,) + tuple(shape), lambda *_: (l,) + (0,) * nd,
                            pipeline_mode=pl.Buffered(1))
    return make


def _params(sem):
    return pltpu.CompilerParams(dimension_semantics=sem, vmem_limit_bytes=VMEM_LIMIT_BYTES)


def _ffn_kernel(*refs, has_pre):
    if has_pre:
        o_ref, wo_ref, x_ref, g_ref, wg_ref, wu_ref, wd_ref, out_ref = refs
        x = x_ref[...] + jnp.dot(o_ref[...], wo_ref[...], preferred_element_type=F32)
    else:
        x_ref, g_ref, wg_ref, wu_ref, wd_ref, out_ref = refs
        x = x_ref[...]
    h = _rms(x, g_ref[...]).astype(BF16)
    d_ff = wg_ref.shape[1]
    y = jnp.zeros_like(x)
    for c in range(d_ff // FFN_FC):
        sl = slice(c * FFN_FC, (c + 1) * FFN_FC)
        gate = jnp.dot(h, wg_ref[:, sl], preferred_element_type=F32)
        up = jnp.dot(h, wu_ref[:, sl], preferred_element_type=F32)
        a = (gate * jax.nn.sigmoid(gate) * up).astype(BF16)
        y = y + jnp.dot(a, wd_ref[sl, :], preferred_element_type=F32)
    out_ref[...] = x + 0.5 * y


def _ffn(x2d, gain, wg, wu, wd, layer, pre=None):
    n, d = x2d.shape
    d_ff = wg.shape[-1]
    row = lambda i: (i, 0)
    in_specs = [pl.BlockSpec((FFN_TM, d), row),
                _layer_weight((1, d))(layer),
                _layer_weight((d, d_ff))(layer),
                _layer_weight((d, d_ff))(layer),
                _layer_weight((d_ff, d))(layer)]
    args = [x2d, gain, wg, wu, wd]
    if pre is not None:
        o2d, wo, wo_layer = pre
        in_specs = [pl.BlockSpec((FFN_TM, d), row), _layer_weight((d, d))(wo_layer)] + in_specs
        args = [o2d, wo] + args
    return pl.pallas_call(
        functools.partial(_ffn_kernel, has_pre=pre is not None),
        out_shape=jax.ShapeDtypeStruct((n, d), F32),
        grid=(n // FFN_TM,),
        in_specs=in_specs,
        out_specs=pl.BlockSpec((FFN_TM, d), row),
        compiler_params=_params(("parallel",)),
        name="ffn_pre" if pre is not None else "ffn",
    )(*args)


def _head_norm_t(xt, gain_col):
    return xt * lax.rsqrt(jnp.mean(xt * xt, axis=0, keepdims=True) + EPS) * gain_col


def _split3(c):
    hi = c.astype(BF16).astype(F32)
    mid = (c - hi).astype(BF16).astype(F32)
    lo = (c - hi - mid).astype(BF16).astype(F32)
    return hi, mid, lo


def _fox_proj_kernel(x_ref, g_ref, wqkv_ref, wf_ref, bf_ref, qg_ref, kg_ref, tri_ref,
                     q_out, k_out, v_out, carry_ref):
    tm = x_ref.shape[0]

    @pl.when(pl.program_id(1) == 0)
    def _():
        carry_ref[...] = jnp.zeros_like(carry_ref)

    h = _rms(x_ref[...], g_ref[...]).astype(BF16)
    qkv = lax.dot_general(wqkv_ref[...], h, NT_DIMS, preferred_element_type=F32)
    for c in range(ATT_T // ATT_TK):
        v_out[c] = qkv[2 * D_MODEL:, c * ATT_TK:(c + 1) * ATT_TK].astype(BF16)
    f = lax.dot_general(wf_ref[...], h, NT_DIMS, preferred_element_type=F32) + bf_ref[...]
    logf = jax.nn.log_sigmoid(f)
    tri = tri_ref[...]
    cum = carry_ref[...]
    for piece in _split3(logf):
        cum = cum + jnp.dot(piece.astype(BF16), tri, preferred_element_type=F32)
    carry_ref[...] = cum[:, tm - 1:tm]

    row = lax.broadcasted_iota(jnp.int32, (AUG_ROWS, tm), 0)
    tail = SLAB - HEAD_DIM - AUG_ROWS
    for hh in range(FOX_HEADS):
        c_hi, c_mid, c_lo = _split3(cum[hh:hh + 1, :] * LOG2E)
        q_aug = jnp.where(row < 3, 1.0,
                          jnp.where(row == 3, c_hi, jnp.where(row == 4, c_mid,
                                                              jnp.where(row == 5, c_lo, 0.0))))
        k_aug = jnp.where(row == 0, -c_hi,
                          jnp.where(row == 1, -c_mid, jnp.where(row == 2, -c_lo,
                                                                jnp.where(row < 6, 1.0, 0.0))))
        qt = qkv[hh * HEAD_DIM:(hh + 1) * HEAD_DIM, :]
        kt = qkv[D_MODEL + hh * HEAD_DIM:D_MODEL + (hh + 1) * HEAD_DIM, :]
        q_out[hh, 0:HEAD_DIM, :] = (_head_norm_t(qt, qg_ref[...]) * QSCALE).astype(BF16)
        q_out[hh, HEAD_DIM:HEAD_DIM + AUG_ROWS, :] = q_aug.astype(BF16)
        q_out[hh, HEAD_DIM + AUG_ROWS:SLAB, :] = jnp.zeros((tail, tm), BF16)
        k_slab = jnp.concatenate([_head_norm_t(kt, kg_ref[...]), k_aug, jnp.zeros((tail, tm), F32)],
                                 axis=0)
        k_out[:, hh * SLAB:(hh + 1) * SLAB] = k_slab.T.astype(BF16)


def _fox_proj(x, gain, wqkv_t, wf_t, b_f, q_gain, k_gain, tri, layer, idx):
    b, s, d = x.shape
    nt = s // ATT_T
    tile = lambda bi, i: (bi, i, 0)
    return pl.pallas_call(
        _fox_proj_kernel,
        out_shape=(jax.ShapeDtypeStruct((b, nt, FOX_HEADS, SLAB, ATT_T), BF16),
                   jax.ShapeDtypeStruct((b, s, FOX_HEADS * SLAB), BF16),
                   jax.ShapeDtypeStruct((b, s // ATT_TK, d, ATT_TK), BF16)),
        grid=(b, nt),
        in_specs=[pl.BlockSpec((None, ATT_T, d), tile),
                  _layer_weight((1, d))(layer),
                  _layer_weight((3 * d, d))(idx),
                  _layer_weight((FOX_HEADS, d))(idx),
                  _layer_weight((FOX_HEADS, 1))(idx),
                  _layer_weight((HEAD_DIM, 1))(idx),
                  _layer_weight((HEAD_DIM, 1))(idx),
                  _resident((ATT_T, ATT_T))],
        out_specs=(pl.BlockSpec((None, None, FOX_HEADS, SLAB, ATT_T), lambda bi, i: (bi, i, 0, 0, 0)),
                   pl.BlockSpec((None, ATT_T, FOX_HEADS * SLAB), tile),
                   pl.BlockSpec((None, ATT_T // ATT_TK, d, ATT_TK), lambda bi, i: (bi, i, 0, 0))),
        scratch_shapes=[pltpu.VMEM((FOX_HEADS, 1), F32)],
        compiler_params=_params(("parallel", "arbitrary")),
        name="fox_proj",
    )(x, gain, wqkv_t, wf_t, b_f, q_gain, k_gain, tri)


def _attend(q_of, k_of, v_of, s_ref, m_ref, l_ref, acc_ref, qi, visible):
    sub = ATT_T // ATT_TK
    assert sub % 2 == 0
    n_full = qi * sub
    m_ref[...] = jnp.full_like(m_ref, -1e30)
    l_ref[...] = jnp.zeros_like(l_ref)
    acc_ref[...] = jnp.zeros_like(acc_ref)

    def scores(ki, slot):
        for j in range(2):
            s_ref[slot, j] = jnp.dot(k_of(ki, j), q_of(j), preferred_element_type=F32)

    def update(ki, slot, mask):
        for j in range(2):
            s = s_ref[slot, j]
            if mask is not None:
                s = jnp.where(mask, s, -jnp.inf)
            m_prev = m_ref[j]
            m_new = jnp.maximum(m_prev, jnp.max(s, axis=0, keepdims=True))
            alpha = jnp.exp2(m_prev - m_new)
            p = jnp.exp2(s - m_new)
            l_ref[j] = alpha * l_ref[j] + jnp.sum(p, axis=0, keepdims=True)
            acc_ref[j] = alpha * acc_ref[j] + jnp.dot(v_of(ki, j), p.astype(BF16),
                                                      preferred_element_type=F32)
            m_ref[j] = m_new

    def body(i, carry):
        for c in range(sub):
            ki = i * sub + c
            scores(ki + 1, (c + 1) % 2)
            update(ki, c % 2, None)
        return carry

    scores(0, 0)
    lax.fori_loop(0, qi, body, 0)
    for c in range(sub):
        if c + 1 < sub:
            scores(n_full + c + 1, (c + 1) % 2)
        update(n_full + c, c % 2, visible(c * ATT_TK))


def _attn_scratch(dv):
    return [pltpu.VMEM((2, 2, ATT_TK, ATT_T), F32),
            pltpu.VMEM((2, 1, ATT_T), F32),
            pltpu.VMEM((2, 1, ATT_T), F32),
            pltpu.VMEM((2, dv, ATT_T), F32)]


def _key_rows(ki):
    return pl.ds(pl.multiple_of(ki * ATT_TK, ATT_TK), ATT_TK)


def _key_query_ids(key0, t):
    key = key0 + lax.broadcasted_iota(jnp.int32, (ATT_TK, t), 0)
    qry = lax.broadcasted_iota(jnp.int32, (ATT_TK, t), 1)
    return key, qry


def _fox_attn_kernel(q_ref, k_ref, v_ref, o_ref, s_ref, m_ref, l_ref, acc_ref):
    t = o_ref.shape[0]
    qi = pl.program_id(2)

    def causal(key0):
        key, qry = _key_query_ids(key0, t)
        return key <= qry

    _attend(lambda j: q_ref[j],
            lambda ki, j: k_ref[_key_rows(ki), j * SLAB:(j + 1) * SLAB],
            lambda ki, j: v_ref[ki, j * HEAD_DIM:(j + 1) * HEAD_DIM, :],
            s_ref, m_ref, l_ref, acc_ref, qi, causal)
    o_t = jnp.concatenate([acc_ref[0] / l_ref[0], acc_ref[1] / l_ref[1]], axis=0)
    o_ref[...] = o_t.T.astype(BF16)


def _fox_attn(q_t, k, v_t):
    b, nk, d, _ = v_t.shape
    s = nk * ATT_TK
    nt = s // ATT_T
    pair = 2 * HEAD_DIM
    return pl.pallas_call(
        _fox_attn_kernel,
        out_shape=jax.ShapeDtypeStruct((b, s, d), BF16),
        grid=(b, FOX_HEADS // 2, nt),
        in_specs=[pl.BlockSpec((None, None, 2, SLAB, ATT_T), lambda bi, hp, qi: (bi, qi, hp, 0, 0)),
                  pl.BlockSpec((None, s, 2 * SLAB), lambda bi, hp, qi: (bi, 0, hp)),
                  pl.BlockSpec((None, nk, pair, ATT_TK), lambda bi, hp, qi: (bi, 0, hp, 0))],
        out_specs=pl.BlockSpec((None, ATT_T, pair), lambda bi, hp, qi: (bi, qi, hp)),
        scratch_shapes=_attn_scratch(HEAD_DIM),
        compiler_params=_params(("parallel", "parallel", "arbitrary")),
        name="fox_attn",
    )(q_t, k, v_t)


def _diff_proj_kernel(x_ref, pos_ref, g_ref, wqkv_ref, qg_ref, kg_ref, invf_ref,
                      q_out, k_out, v_out):
    half = ROT_DIM // 2
    h = _rms(x_ref[...], g_ref[...]).astype(BF16)
    qkv = lax.dot_general(wqkv_ref[...], h, NT_DIMS, preferred_element_type=F32)
    for c in range(ATT_T // ATT_TK):
        v_out[c] = qkv[2 * D_MODEL:, c * ATT_TK:(c + 1) * ATT_TK].astype(BF16)
    ang = invf_ref[...] * pos_ref[...].astype(F32)
    cos = jnp.cos(ang)
    sin = jnp.sin(ang)

    def rope_t(xt):
        x1, x2 = xt[0:half], xt[half:ROT_DIM]
        return jnp.concatenate([x1 * cos - x2 * sin, x2 * cos + x1 * sin, xt[ROT_DIM:]], axis=0)

    def head_t(base, gain_ref):
        maps = [rope_t(_head_norm_t(qkv[base + mp * HEAD_DIM:base + (mp + 1) * HEAD_DIM, :],
                                    gain_ref[...])) for mp in range(2)]
        return jnp.concatenate(maps, axis=0)

    for hh in range(DIFF_HEADS):
        q_out[hh] = (head_t(hh * SLAB, qg_ref) * QSCALE).astype(BF16)
        k_out[:, hh * SLAB:(hh + 1) * SLAB] = head_t(D_MODEL + hh * SLAB, kg_ref).T.astype(BF16)


def _diff_proj(x, pos, gain, wqkv_t, q_gain, k_gain, inv_freq, layer, idx):
    b, s, d = x.shape
    nt = s // ATT_T
    tile = lambda bi, i: (bi, i, 0)
    return pl.pallas_call(
        _diff_proj_kernel,
        out_shape=(jax.ShapeDtypeStruct((b, nt, DIFF_HEADS, SLAB, ATT_T), BF16),
                   jax.ShapeDtypeStruct((b, s, d), BF16),
                   jax.ShapeDtypeStruct((b, s // ATT_TK, d, ATT_TK), BF16)),
        grid=(b, nt),
        in_specs=[pl.BlockSpec((None, ATT_T, d), tile),
                  pl.BlockSpec((None, 1, ATT_T), lambda bi, i: (bi, 0, i)),
                  _layer_weight((1, d))(layer),
                  _layer_weight((3 * d, d))(idx),
                  _layer_weight((HEAD_DIM, 1))(idx),
                  _layer_weight((HEAD_DIM, 1))(idx),
                  _resident((ROT_DIM // 2, 1))],
        out_specs=(pl.BlockSpec((None, None, DIFF_HEADS, SLAB, ATT_T), lambda bi, i: (bi, i, 0, 0, 0)),
                   pl.BlockSpec((None, ATT_T, d), tile),
                   pl.BlockSpec((None, ATT_T // ATT_TK, d, ATT_TK), lambda bi, i: (bi, i, 0, 0))),
        compiler_params=_params(("parallel", "parallel")),
        name="diff_proj",
    )(x, pos, gain, wqkv_t, q_gain, k_gain, inv_freq)


def _diff_attn_kernel(q_ref, k_ref, v_ref, lq1_ref, lk1_ref, lq2_ref, lk2_ref, sg_ref, o_ref,
                      qm_ref, s_ref, m_ref, l_ref, acc_ref, *, lambda_init):
    t = o_ref.shape[0]
    qi = pl.program_id(2)
    q = q_ref[...].astype(F32)
    feat = lax.broadcasted_iota(jnp.int32, (SLAB, t), 0)
    qm_ref[0] = jnp.where(feat < HEAD_DIM, q, 0.0).astype(BF16)
    qm_ref[1] = jnp.where(feat >= HEAD_DIM, q, 0.0).astype(BF16)

    def chunk_causal(key0):
        key, qry = _key_query_ids(key0, t)
        return (key // CHUNK) <= (qry // CHUNK)

    _attend(lambda j: qm_ref[j],
            lambda ki, j: k_ref[_key_rows(ki), :],
            lambda ki, j: v_ref[ki],
            s_ref, m_ref, l_ref, acc_ref, qi, chunk_causal)
    lam = (jnp.exp(jnp.sum(lq1_ref[...] * lk1_ref[...], axis=1, keepdims=True))
           - jnp.exp(jnp.sum(lq2_ref[...] * lk2_ref[...], axis=1, keepdims=True)) + lambda_init)
    o_t = acc_ref[0] / l_ref[0] - lam * (acc_ref[1] / l_ref[1])
    o_t = o_t * lax.rsqrt(jnp.mean(o_t * o_t, axis=0, keepdims=True) + EPS) * sg_ref[...]
    o_ref[...] = (o_t * (1.0 - lambda_init)).T.astype(BF16)


def _diff_attn(q_t, k, v_t, lq1, lk1, lq2, lk2, sub_gain, idx, lambda_init):
    b, nk, d, _ = v_t.shape
    s = nk * ATT_TK
    nt = s // ATT_T
    vec = _layer_weight((1, HEAD_DIM))(idx)
    return pl.pallas_call(
        functools.partial(_diff_attn_kernel, lambda_init=lambda_init),
        out_shape=jax.ShapeDtypeStruct((b, s, d), BF16),
        grid=(b, DIFF_HEADS, nt),
        in_specs=[pl.BlockSpec((None, None, None, SLAB, ATT_T), lambda bi, hh, qi: (bi, qi, hh, 0, 0)),
                  pl.BlockSpec((None, s, SLAB), lambda bi, hh, qi: (bi, 0, hh)),
                  pl.BlockSpec((None, nk, SLAB, ATT_TK), lambda bi, hh, qi: (bi, 0, hh, 0)),
                  vec, vec, vec, vec,
                  _layer_weight((2 * HEAD_DIM, 1))(idx)],
        out_specs=pl.BlockSpec((None, ATT_T, SLAB), lambda bi, hh, qi: (bi, qi, hh)),
        scratch_shapes=[pltpu.VMEM((2, SLAB, ATT_T), BF16)] + _attn_scratch(2 * HEAD_DIM),
        compiler_params=_params(("parallel", "parallel", "arbitrary")),
        name="diff_attn",
    )(q_t, k, v_t, lq1, lk1, lq2, lk2, sub_gain)


def _conv_kernel(x_ref, g_ref, w1_ref, b1_ref, wdw_ref, bdw_ref, lng_ref, lnb_ref, w2_ref, b2_ref,
                 out_ref, u_ref, a_ref):
    tm, d = x_ref.shape

    @pl.when(pl.program_id(1) == 0)
    def _():
        u_ref[0:CONV_HALO, :] = jnp.zeros((CONV_HALO, d), F32)

    @pl.when(pl.program_id(1) > 0)
    def _():
        u_ref[0:CONV_HALO, :] = u_ref[tm:tm + CONV_HALO, :]

    x = x_ref[...]
    h = _rms(x, g_ref[...]).astype(BF16)
    u = jnp.dot(h, w1_ref[...], preferred_element_type=F32) + b1_ref[...]
    u_ref[CONV_HALO:CONV_HALO + tm, :] = u[:, :d] * jax.nn.sigmoid(u[:, d:])

    first = CONV_HALO - (CONV_WIDTH - 1)
    for c in range(tm // CONV_RC):
        base = c * CONV_RC
        acc = jnp.broadcast_to(bdw_ref[...], (CONV_RC, d))
        for j in range(CONV_WIDTH):
            acc = acc + wdw_ref[j:j + 1, :] * u_ref[base + first + j:base + first + j + CONV_RC, :]
        mu = jnp.mean(acc, axis=-1, keepdims=True)
        cen = acc - mu
        var = jnp.mean(cen * cen, axis=-1, keepdims=True)
        y = cen * lax.rsqrt(var + EPS) * lng_ref[...] + lnb_ref[...]
        a_ref[base:base + CONV_RC, :] = (y * jax.nn.sigmoid(y)).astype(BF16)

    out_ref[...] = x + jnp.dot(a_ref[...], w2_ref[...], preferred_element_type=F32) + b2_ref[...]


def _conv_module(x, gain, w1, b1, wdw, bdw, lng, lnb, w2, b2, layer, idx):
    b, s, d = x.shape
    tile = pl.BlockSpec((None, CONV_TM, d), lambda bi, i: (bi, i, 0))
    vec = _layer_weight((1, d))(idx)
    return pl.pallas_call(
        _conv_kernel,
        out_shape=jax.ShapeDtypeStruct((b, s, d), F32),
        grid=(b, s // CONV_TM),
        in_specs=[tile,
                  _layer_weight((1, d))(layer),
                  _layer_weight((d, 2 * d))(idx),
                  _layer_weight((1, 2 * d))(idx),
                  _layer_weight((CONV_WIDTH, d))(idx),
                  vec, vec, vec,
                  _layer_weight((d, d))(idx),
                  vec],
        out_specs=tile,
        scratch_shapes=[pltpu.VMEM((CONV_HALO + CONV_TM, d), F32),
                        pltpu.VMEM((CONV_TM, d), BF16)],
        compiler_params=_params(("parallel", "arbitrary")),
        name="conv_module",
    )(x, gain, w1, b1, wdw, bdw, lng, lnb, w2, b2)


def kernel(x, positions, norm_ffn1, norm_mix, norm_ffn2, ffn1_w_gate, ffn1_w_up, ffn1_w_down, ffn2_w_gate, ffn2_w_up, ffn2_w_down, fox_w_in, fox_b_f, fox_q_gain, fox_k_gain, fox_w_out, conv_w_pw1, conv_b_pw1, conv_w_dw, conv_b_dw, conv_ln_g, conv_ln_b, conv_w_pw2, conv_b_pw2, diff_w_in, diff_q_gain, diff_k_gain, diff_lambda_q1, diff_lambda_k1, diff_lambda_q2, diff_lambda_k2, diff_sub_gain, diff_w_out):
    b, s, d = x.shape
    depth = norm_ffn1.shape[0]
    assert d == D_MODEL and s % ATT_T == 0 and s % CONV_TM == 0 and (b * s) % FFN_TM == 0

    bf = lambda w: w.astype(BF16)
    row = lambda v: v[:, None, :]
    col = lambda v: v[:, :, None]
    g1, gm, g2 = row(norm_ffn1), row(norm_mix), row(norm_ffn2)
    f1 = (bf(ffn1_w_gate), bf(ffn1_w_up), bf(ffn1_w_down))
    f2 = (bf(ffn2_w_gate), bf(ffn2_w_up), bf(ffn2_w_down))
    fox_wqkv_t = bf(jnp.swapaxes(fox_w_in[:, :, :3 * d], 1, 2))
    fox_wf_t = bf(jnp.swapaxes(fox_w_in[:, :, 3 * d:], 1, 2))
    fox_wo = bf(fox_w_out)
    diff_wqkv_t = bf(jnp.swapaxes(diff_w_in, 1, 2))
    diff_wo = bf(diff_w_out)
    conv_w1, conv_w2 = bf(conv_w_pw1), bf(conv_w_pw2)
    tri = (jnp.arange(ATT_T)[:, None] <= jnp.arange(ATT_T)[None, :]).astype(BF16)
    inv_freq = (ROPE_THETA ** (-jnp.arange(0, ROT_DIM, 2, dtype=F32) / ROT_DIM))[:, None]
    pos = positions[:, None, :]

    flat = lambda a: a.reshape(b * s, d)
    i_fox = i_conv = i_diff = 0
    x2 = flat(x)
    for i in range(depth):
        x2 = _ffn(x2, g1, *f1, i)
        x3 = x2.reshape(b, s, d)
        kind = i % N_MIXERS
        if kind == 0:
            q_t, k, v_t = _fox_proj(x3, gm, fox_wqkv_t, fox_wf_t, col(fox_b_f),
                                    col(fox_q_gain), col(fox_k_gain), tri, i, i_fox)
            o = _fox_attn(q_t, k, v_t)
            x2 = _ffn(x2, g2, *f2, i, pre=(flat(o), fox_wo, i_fox))
            i_fox += 1
        elif kind == 1:
            x3 = _conv_module(x3, gm, conv_w1, row(conv_b_pw1), conv_w_dw, row(conv_b_dw),
                              row(conv_ln_g), row(conv_ln_b), conv_w2, row(conv_b_pw2), i, i_conv)
            x2 = _ffn(flat(x3), g2, *f2, i)
            i_conv += 1
        else:
            lambda_init = 0.8 - 0.6 * math.exp(-0.3 * i)
            q_t, k, v_t = _diff_proj(x3, pos, gm, diff_wqkv_t, col(diff_q_gain),
                                     col(diff_k_gain), inv_freq, i, i_diff)
            o = _diff_attn(q_t, k, v_t, row(diff_lambda_q1), row(diff_lambda_k1),
                           row(diff_lambda_q2), row(diff_lambda_k2), col(diff_sub_gain),
                           i_diff, lambda_init)
            x2 = _ffn(x2, g2, *f2, i, pre=(flat(o), diff_wo, i_diff))
            i_diff += 1
    return x2.reshape(b, s, d)
```

```python
import functools
import math

import jax
import jax.numpy as jnp
from jax import lax
from jax.experimental import pallas as pl
from jax.experimental.pallas import tpu as pltpu

D_MODEL = 1024
HEAD_DIM = 64
FOX_HEADS = D_MODEL // HEAD_DIM
DIFF_HEADS = D_MODEL // (2 * HEAD_DIM)
ROT_DIM = HEAD_DIM // 4
ROPE_THETA = 500000.0
CONV_WIDTH = 31
CHUNK = 64
N_MIXERS = 3
EPS = 1e-6
LOG2E = math.log2(math.e)
QSCALE = HEAD_DIM ** -0.5 * LOG2E

VMEM_LIMIT_BYTES = 56 * 1024 * 1024
SUBLANES = 8
SLAB = 128
AUG_ROWS = 16

FFN_TM = 512
ATT_T = 512
ATT_TK = 256
ATT_TRIP = 4
CONV_TM = 256
CONV_RC = 32
CONV_HALO = 32
FFN_FC = 256

F32 = jnp.float32
BF16 = jnp.bfloat16
NT_DIMS = (((1,), (1,)), ((), ()))


def _rms(x, g):
    return x * lax.rsqrt(jnp.mean(x * x, axis=-1, keepdims=True) + EPS) * g


def _resident(shape):
    nd = len(shape)
    return pl.BlockSpec(shape, lambda *_: (0,) * nd, pipeline_mode=pl.Buffered(1))


def _layer_weight(shape):
    nd = len(shape)

    def make(l):
        return pl.BlockSpec((None,) + tuple(shape), lambda *_: (l,) + (0,) * nd,
                            pipeline_mode=pl.Buffered(1))
    return make


def _params(sem):
    return pltpu.CompilerParams(dimension_semantics=sem, vmem_limit_bytes=VMEM_LIMIT_BYTES)


def _ffn_kernel(*refs, has_pre):
    if has_pre:
        o_ref, wo_ref, x_ref, g_ref, wg_ref, wu_ref, wd_ref, out_ref = refs
        x = x_ref[...] + jnp.dot(o_ref[...], wo_ref[...], preferred_element_type=F32)
    else:
        x_ref, g_ref, wg_ref, wu_ref, wd_ref, out_ref = refs
        x = x_ref[...]
    h = _rms(x, g_ref[...]).astype(BF16)
    d_ff = wg_ref.shape[1]
    y = jnp.zeros_like(x)
    for c in range(d_ff // FFN_FC):
        sl = slice(c * FFN_FC, (c + 1) * FFN_FC)
        gate = jnp.dot(h, wg_ref[:, sl], preferred_element_type=F32)
        up = jnp.dot(h, wu_ref[:, sl], preferred_element_type=F32)
        a = (gate * jax.nn.sigmoid(gate) * up).astype(BF16)
        y = y + jnp.dot(a, wd_ref[sl, :], preferred_element_type=F32)
    out_ref[...] = x + 0.5 * y


def _ffn(x2d, gain, wg, wu, wd, layer, pre=None):
    n, d = x2d.shape
    d_ff = wg.shape[-1]
    row = lambda i: (i, 0)
    in_specs = [pl.BlockSpec((FFN_TM, d), row),
                _layer_weight((1, d))(layer),
                _layer_weight((d, d_ff))(layer),
                _layer_weight((d, d_ff))(layer),
                _layer_weight((d_ff, d))(layer)]
    args = [x2d, gain, wg, wu, wd]
    if pre is not None:
        o2d, wo, wo_layer = pre
        in_specs = [pl.BlockSpec((FFN_TM, d), row), _layer_weight((d, d))(wo_layer)] + in_specs
        args = [o2d, wo] + args
    return pl.pallas_call(
        functools.partial(_ffn_kernel, has_pre=pre is not None),
        out_shape=jax.ShapeDtypeStruct((n, d), F32),
        grid=(n // FFN_TM,),
        in_specs=in_specs,
        out_specs=pl.BlockSpec((FFN_TM, d), row),
        compiler_params=_params(("parallel",)),
        name="ffn_pre" if pre is not None else "ffn",
    )(*args)


def _head_norm_t(xt, gain_col):
    return xt * lax.rsqrt(jnp.mean(xt * xt, axis=0, keepdims=True) + EPS) * gain_col


def _split3(c):
    hi = c.astype(BF16).astype(F32)
    mid = (c - hi).astype(BF16).astype(F32)
    lo = (c - hi - mid).astype(BF16).astype(F32)
    return hi, mid, lo


def _fox_proj_kernel(x_ref, g_ref, wqkv_ref, wf_ref, bf_ref, qg_ref, kg_ref, tri_ref,
                     q_out, k_out, v_out, carry_ref):
    tm = x_ref.shape[0]

    @pl.when(pl.program_id(1) == 0)
    def _():
        carry_ref[...] = jnp.zeros_like(carry_ref)

    h = _rms(x_ref[...], g_ref[...]).astype(BF16)
    qkv = lax.dot_general(wqkv_ref[...], h, NT_DIMS, preferred_element_type=F32)
    for c in range(ATT_T // ATT_TK):
        v_out[c] = qkv[2 * D_MODEL:, c * ATT_TK:(c + 1) * ATT_TK].astype(BF16)
    f = lax.dot_general(wf_ref[...], h, NT_DIMS, preferred_element_type=F32) + bf_ref[...]
    logf = jax.nn.log_sigmoid(f)
    tri = tri_ref[...]
    cum = carry_ref[...]
    for piece in _split3(logf):
        cum = cum + jnp.dot(piece.astype(BF16), tri, preferred_element_type=F32)
    carry_ref[...] = cum[:, tm - 1:tm]

    row = lax.broadcasted_iota(jnp.int32, (AUG_ROWS, tm), 0)
    tail = SLAB - HEAD_DIM - AUG_ROWS
    for hh in range(FOX_HEADS):
        c_hi, c_mid, c_lo = _split3(cum[hh:hh + 1, :] * LOG2E)
        q_aug = jnp.where(row < 3, 1.0,
                          jnp.where(row == 3, c_hi, jnp.where(row == 4, c_mid,
                                                              jnp.where(row == 5, c_lo, 0.0))))
        k_aug = jnp.where(row == 0, -c_hi,
                          jnp.where(row == 1, -c_mid, jnp.where(row == 2, -c_lo,
                                                                jnp.where(row < 6, 1.0, 0.0))))
        qt = qkv[hh * HEAD_DIM:(hh + 1) * HEAD_DIM, :]
        kt = qkv[D_MODEL + hh * HEAD_DIM:D_MODEL + (hh + 1) * HEAD_DIM, :]
        q_out[hh, 0:HEAD_DIM, :] = (_head_norm_t(qt, qg_ref[...]) * QSCALE).astype(BF16)
        q_out[hh, HEAD_DIM:HEAD_DIM + AUG_ROWS, :] = q_aug.astype(BF16)
        q_out[hh, HEAD_DIM + AUG_ROWS:SLAB, :] = jnp.zeros((tail, tm), BF16)
        k_slab = jnp.concatenate([_head_norm_t(kt, kg_ref[...]), k_aug, jnp.zeros((tail, tm), F32)],
                                 axis=0)
        k_out[:, hh * SLAB:(hh + 1) * SLAB] = k_slab.T.astype(BF16)


def _fox_proj(x, gain, wqkv_t, wf_t, b_f, q_gain, k_gain, tri, layer, idx):
    b, s, d = x.shape
    nt = s // ATT_T
    tile = lambda bi, i: (bi, i, 0)
    return pl.pallas_call(
        _fox_proj_kernel,
        out_shape=(jax.ShapeDtypeStruct((b, nt, FOX_HEADS, SLAB, ATT_T), BF16),
                   jax.ShapeDtypeStruct((b, s, FOX_HEADS * SLAB), BF16),
                   jax.ShapeDtypeStruct((b, s // ATT_TK, d, ATT_TK), BF16)),
        grid=(b, nt),
        in_specs=[pl.BlockSpec((None, ATT_T, d), tile),
                  _layer_weight((1, d))(layer),
                  _layer_weight((3 * d, d))(idx),
                  _layer_weight((FOX_HEADS, d))(idx),
                  _layer_weight((FOX_HEADS, 1))(idx),
                  _layer_weight((HEAD_DIM, 1))(idx),
                  _layer_weight((HEAD_DIM, 1))(idx),
                  _resident((ATT_T, ATT_T))],
        out_specs=(pl.BlockSpec((None, None, FOX_HEADS, SLAB, ATT_T), lambda bi, i: (bi, i, 0, 0, 0)),
                   pl.BlockSpec((None, ATT_T, FOX_HEADS * SLAB), tile),
                   pl.BlockSpec((None, ATT_T // ATT_TK, d, ATT_TK), lambda bi, i: (bi, i, 0, 0))),
        scratch_shapes=[pltpu.VMEM((FOX_HEADS, 1), F32)],
        compiler_params=_params(("parallel", "arbitrary")),
        name="fox_proj",
    )(x, gain, wqkv_t, wf_t, b_f, q_gain, k_gain, tri)


def _attend(q_of, k_of, v_of, s_ref, m_ref, l_ref, acc_ref, qi, visible):
    sub = ATT_T // ATT_TK
    assert sub % 2 == 0 and ATT_TRIP % 2 == 0 and ATT_TRIP % sub == 0
    n_full = qi * sub
    m_ref[...] = jnp.full_like(m_ref, -1e30)
    l_ref[...] = jnp.zeros_like(l_ref)
    acc_ref[...] = jnp.zeros_like(acc_ref)

    def scores(slot, ki, c0=0):
        for j in range(2):
            s_ref[slot, j, :, c0:] = jnp.dot(k_of(ki, j), q_of(j, c0),
                                             preferred_element_type=F32)

    def update(slot, ki, mask=None, c0=0):
        for j in range(2):
            s = s_ref[slot, j, :, c0:]
            if mask is not None:
                s = jnp.where(mask, s, -jnp.inf)
            m_prev = m_ref[j, :, c0:]
            m_new = jnp.maximum(m_prev, jnp.max(s, axis=0, keepdims=True))
            alpha = jnp.exp2(m_prev - m_new)
            p = jnp.exp2(s - m_new)
            l_ref[j, :, c0:] = alpha * l_ref[j, :, c0:] + jnp.sum(p, axis=0, keepdims=True)
            acc_ref[j, :, c0:] = alpha * acc_ref[j, :, c0:] + jnp.dot(
                v_of(ki, j), p.astype(BF16), preferred_element_type=F32)
            m_ref[j, :, c0:] = m_new

    def run(first, count):
        for c in range(count):
            scores((c + 1) % 2, first + c + 1)
            update(c % 2, first + c)

    def body(i, carry):
        run(i * ATT_TRIP, ATT_TRIP)
        return carry

    scores(0, 0)
    trips = n_full // ATT_TRIP
    lax.fori_loop(0, trips, body, 0)
    for rem in range(sub, ATT_TRIP, sub):
        @pl.when(n_full - trips * ATT_TRIP == rem)
        def _():
            run(trips * ATT_TRIP, rem)
    for c in range(sub):
        if c + 1 < sub:
            scores((c + 1) % 2, n_full + c + 1, (c + 1) * ATT_TK)
        update(c % 2, n_full + c, visible(c * ATT_TK, c * ATT_TK), c * ATT_TK)


def _attn_scratch(dv):
    return [pltpu.VMEM((2, 2, ATT_TK, ATT_T), F32),
            pltpu.VMEM((2, 1, ATT_T), F32),
            pltpu.VMEM((2, 1, ATT_T), F32),
            pltpu.VMEM((2, dv, ATT_T), F32)]


def _key_rows(ki):
    return pl.ds(pl.multiple_of(ki * ATT_TK, ATT_TK), ATT_TK)


def _key_query_ids(key0, c0):
    shape = (ATT_TK, ATT_T - c0)
    key = key0 + lax.broadcasted_iota(jnp.int32, shape, 0)
    qry = c0 + lax.broadcasted_iota(jnp.int32, shape, 1)
    return key, qry


def _fox_attn_kernel(q_ref, k_ref, v_ref, o_ref, s_ref, m_ref, l_ref, acc_ref):
    qi = pl.program_id(2)

    def causal(key0, c0):
        key, qry = _key_query_ids(key0, c0)
        return key <= qry

    _attend(lambda j, c0: q_ref[j, :, c0:],
            lambda ki, j: k_ref[_key_rows(ki), j * SLAB:(j + 1) * SLAB],
            lambda ki, j: v_ref[ki, j * HEAD_DIM:(j + 1) * HEAD_DIM, :],
            s_ref, m_ref, l_ref, acc_ref, qi, causal)
    o_t = jnp.concatenate([acc_ref[0] / l_ref[0], acc_ref[1] / l_ref[1]], axis=0)
    o_ref[...] = o_t.T.astype(BF16)


def _fox_attn(q_t, k, v_t):
    b, nk, d, _ = v_t.shape
    s = nk * ATT_TK
    nt = s // ATT_T
    pair = 2 * HEAD_DIM
    return pl.pallas_call(
        _fox_attn_kernel,
        out_shape=jax.ShapeDtypeStruct((b, s, d), BF16),
        grid=(b, FOX_HEADS // 2, nt),
        in_specs=[pl.BlockSpec((None, None, 2, SLAB, ATT_T), lambda bi, hp, qi: (bi, qi, hp, 0, 0)),
                  pl.BlockSpec((None, s, 2 * SLAB), lambda bi, hp, qi: (bi, 0, hp)),
                  pl.BlockSpec((None, nk, pair, ATT_TK), lambda bi, hp, qi: (bi, 0, hp, 0))],
        out_specs=pl.BlockSpec((None, ATT_T, pair), lambda bi, hp, qi: (bi, qi, hp)),
        scratch_shapes=_attn_scratch(HEAD_DIM),
        compiler_params=_params(("parallel", "parallel", "arbitrary")),
        name="fox_attn",
    )(q_t, k, v_t)


def _diff_proj_kernel(x_ref, pos_ref, g_ref, wqkv_ref, qg_ref, kg_ref, invf_ref,
                      q_out, k_out, v_out):
    half = ROT_DIM // 2
    h = _rms(x_ref[...], g_ref[...]).astype(BF16)
    qkv = lax.dot_general(wqkv_ref[...], h, NT_DIMS, preferred_element_type=F32)
    for c in range(ATT_T // ATT_TK):
        v_out[c] = qkv[2 * D_MODEL:, c * ATT_TK:(c + 1) * ATT_TK].astype(BF16)
    ang = invf_ref[...] * pos_ref[...].astype(F32)
    cos = jnp.cos(ang)
    sin = jnp.sin(ang)

    def rope_t(xt):
        x1, x2 = xt[0:half], xt[half:ROT_DIM]
        return jnp.concatenate([x1 * cos - x2 * sin, x2 * cos + x1 * sin, xt[ROT_DIM:]], axis=0)

    def head_t(base, gain_ref):
        maps = [rope_t(_head_norm_t(qkv[base + mp * HEAD_DIM:base + (mp + 1) * HEAD_DIM, :],
                                    gain_ref[...])) for mp in range(2)]
        return jnp.concatenate(maps, axis=0)

    for hh in range(DIFF_HEADS):
        q_out[hh] = (head_t(hh * SLAB, qg_ref) * QSCALE).astype(BF16)
        k_out[:, hh * SLAB:(hh + 1) * SLAB] = head_t(D_MODEL + hh * SLAB, kg_ref).T.astype(BF16)


def _diff_proj(x, pos, gain, wqkv_t, q_gain, k_gain, inv_freq, layer, idx):
    b, s, d = x.shape
    nt = s // ATT_T
    tile = lambda bi, i: (bi, i, 0)
    return pl.pallas_call(
        _diff_proj_kernel,
        out_shape=(jax.ShapeDtypeStruct((b, nt, DIFF_HEADS, SLAB, ATT_T), BF16),
                   jax.ShapeDtypeStruct((b, s, d), BF16),
                   jax.ShapeDtypeStruct((b, s // ATT_TK, d, ATT_TK), BF16)),
        grid=(b, nt),
        in_specs=[pl.BlockSpec((None, ATT_T, d), tile),
                  pl.BlockSpec((None, 1, ATT_T), lambda bi, i: (bi, 0, i)),
                  _layer_weight((1, d))(layer),
                  _layer_weight((3 * d, d))(idx),
                  _layer_weight((HEAD_DIM, 1))(idx),
                  _layer_weight((HEAD_DIM, 1))(idx),
                  _resident((ROT_DIM // 2, 1))],
        out_specs=(pl.BlockSpec((None, None, DIFF_HEADS, SLAB, ATT_T), lambda bi, i: (bi, i, 0, 0, 0)),
                   pl.BlockSpec((None, ATT_T, d), tile),
                   pl.BlockSpec((None, ATT_T // ATT_TK, d, ATT_TK), lambda bi, i: (bi, i, 0, 0))),
        compiler_params=_params(("parallel", "parallel")),
        name="diff_proj",
    )(x, pos, gain, wqkv_t, q_gain, k_gain, inv_freq)


def _diff_attn_kernel(q_ref, k_ref, v_ref, lq1_ref, lk1_ref, lq2_ref, lk2_ref, sg_ref, o_ref,
                      qm_ref, s_ref, m_ref, l_ref, acc_ref, *, lambda_init):
    t = o_ref.shape[0]
    qi = pl.program_id(2)
    q = q_ref[...].astype(F32)
    feat = lax.broadcasted_iota(jnp.int32, (SLAB, t), 0)
    qm_ref[0] = jnp.where(feat < HEAD_DIM, q, 0.0).astype(BF16)
    qm_ref[1] = jnp.where(feat >= HEAD_DIM, q, 0.0).astype(BF16)

    def chunk_causal(key0, c0):
        key, qry = _key_query_ids(key0, c0)
        return (key // CHUNK) <= (qry // CHUNK)

    _attend(lambda j, c0: qm_ref[j, :, c0:],
            lambda ki, j: k_ref[_key_rows(ki), :],
            lambda ki, j: v_ref[ki],
            s_ref, m_ref, l_ref, acc_ref, qi, chunk_causal)
    lam = (jnp.exp(jnp.sum(lq1_ref[...] * lk1_ref[...], axis=1, keepdims=True))
           - jnp.exp(jnp.sum(lq2_ref[...] * lk2_ref[...], axis=1, keepdims=True)) + lambda_init)
    o_t = acc_ref[0] / l_ref[0] - lam * (acc_ref[1] / l_ref[1])
    o_t = o_t * lax.rsqrt(jnp.mean(o_t * o_t, axis=0, keepdims=True) + EPS) * sg_ref[...]
    o_ref[...] = (o_t * (1.0 - lambda_init)).T.astype(BF16)


def _diff_attn(q_t, k, v_t, lq1, lk1, lq2, lk2, sub_gain, idx, lambda_init):
    b, nk, d, _ = v_t.shape
    s = nk * ATT_TK
    nt = s // ATT_T
    vec = _layer_weight((1, HEAD_DIM))(idx)
    return pl.pallas_call(
        functools.partial(_diff_attn_kernel, lambda_init=lambda_init),
        out_shape=jax.ShapeDtypeStruct((b, s, d), BF16),
        grid=(b, DIFF_HEADS, nt),
        in_specs=[pl.BlockSpec((None, None, None, SLAB, ATT_T), lambda bi, hh, qi: (bi, qi, hh, 0, 0)),
                  pl.BlockSpec((None, s, SLAB), lambda bi, hh, qi: (bi, 0, hh)),
                  pl.BlockSpec((None, nk, SLAB, ATT_TK), lambda bi, hh, qi: (bi, 0, hh, 0)),
                  vec, vec, vec, vec,
                  _layer_weight((2 * HEAD_DIM, 1))(idx)],
        out_specs=pl.BlockSpec((None, ATT_T, SLAB), lambda bi, hh, qi: (bi, qi, hh)),
        scratch_shapes=[pltpu.VMEM((2, SLAB, ATT_T), BF16)] + _attn_scratch(2 * HEAD_DIM),
        compiler_params=_params(("parallel", "parallel", "arbitrary")),
        name="diff_attn",
    )(q_t, k, v_t, lq1, lk1, lq2, lk2, sub_gain)


def _conv_kernel(x_ref, g_ref, w1_ref, b1_ref, wdw_ref, bdw_ref, lng_ref, lnb_ref, w2_ref, b2_ref,
                 out_ref, u_ref, us_ref, a_ref):
    tm, d = x_ref.shape

    @pl.when(pl.program_id(1) == 0)
    def _():
        u_ref[0:CONV_HALO, :] = jnp.zeros((CONV_HALO, d), F32)

    @pl.when(pl.program_id(1) > 0)
    def _():
        u_ref[0:CONV_HALO, :] = u_ref[tm:tm + CONV_HALO, :]

    x = x_ref[...]
    h = _rms(x, g_ref[...]).astype(BF16)
    u = jnp.dot(h, w1_ref[...], preferred_element_type=F32) + b1_ref[...]
    u_ref[CONV_HALO:CONV_HALO + tm, :] = u[:, :d] * jax.nn.sigmoid(u[:, d:])
    span = us_ref.shape[1]
    for b in range(1, SUBLANES):
        us_ref[b - 1] = u_ref[b:b + span, :]

    first = CONV_HALO - (CONV_WIDTH - 1)
    for c in range(tm // CONV_RC):
        base = c * CONV_RC
        acc = jnp.broadcast_to(bdw_ref[...], (CONV_RC, d))
        for j in range(CONV_WIDTH):
            b = (first + j) % SUBLANES
            lo = base + first + j - b
            rows = u_ref[lo:lo + CONV_RC, :] if b == 0 else us_ref[b - 1, lo:lo + CONV_RC, :]
            acc = acc + wdw_ref[j:j + 1, :] * rows
        mu = jnp.mean(acc, axis=-1, keepdims=True)
        cen = acc - mu
        var = jnp.mean(cen * cen, axis=-1, keepdims=True)
        y = cen * lax.rsqrt(var + EPS) * lng_ref[...] + lnb_ref[...]
        a_ref[base:base + CONV_RC, :] = (y * jax.nn.sigmoid(y)).astype(BF16)

    out_ref[...] = x + jnp.dot(a_ref[...], w2_ref[...], preferred_element_type=F32) + b2_ref[...]


def _conv_module(x, gain, w1, b1, wdw, bdw, lng, lnb, w2, b2, layer, idx):
    b, s, d = x.shape
    tile = pl.BlockSpec((None, CONV_TM, d), lambda bi, i: (bi, i, 0))
    vec = _layer_weight((1, d))(idx)
    return pl.pallas_call(
        _conv_kernel,
        out_shape=jax.ShapeDtypeStruct((b, s, d), F32),
        grid=(b, s // CONV_TM),
        in_specs=[tile,
                  _layer_weight((1, d))(layer),
                  _layer_weight((d, 2 * d))(idx),
                  _layer_weight((1, 2 * d))(idx),
                  _layer_weight((CONV_WIDTH, d))(idx),
                  vec, vec, vec,
                  _layer_weight((d, d))(idx),
                  vec],
        out_specs=tile,
        scratch_shapes=[pltpu.VMEM((CONV_HALO + CONV_TM, d), F32),
                        pltpu.VMEM((SUBLANES - 1, CONV_HALO + CONV_TM - SUBLANES, d), F32),
                        pltpu.VMEM((CONV_TM, d), BF16)],
        compiler_params=_params(("parallel", "arbitrary")),
        name="conv_module",
    )(x, gain, w1, b1, wdw, bdw, lng, lnb, w2, b2)


def kernel(x, positions, norm_ffn1, norm_mix, norm_ffn2, ffn1_w_gate, ffn1_w_up, ffn1_w_down, ffn2_w_gate, ffn2_w_up, ffn2_w_down, fox_w_in, fox_b_f, fox_q_gain, fox_k_gain, fox_w_out, conv_w_pw1, conv_b_pw1, conv_w_dw, conv_b_dw, conv_ln_g, conv_ln_b, conv_w_pw2, conv_b_pw2, diff_w_in, diff_q_gain, diff_k_gain, diff_lambda_q1, diff_lambda_k1, diff_lambda_q2, diff_lambda_k2, diff_sub_gain, diff_w_out):
    b, s, d = x.shape
    depth = norm_ffn1.shape[0]
    assert d == D_MODEL and s % ATT_T == 0 and s % CONV_TM == 0 and (b * s) % FFN_TM == 0

    bf = lambda w: w.astype(BF16)
    row = lambda v: v[:, None, :]
    col = lambda v: v[:, :, None]
    g1, gm, g2 = row(norm_ffn1), row(norm_mix), row(norm_ffn2)
    f1 = (bf(ffn1_w_gate), bf(ffn1_w_up), bf(ffn1_w_down))
    f2 = (bf(ffn2_w_gate), bf(ffn2_w_up), bf(ffn2_w_down))
    fox_wqkv_t = bf(jnp.swapaxes(fox_w_in[:, :, :3 * d], 1, 2))
    fox_wf_t = bf(jnp.swapaxes(fox_w_in[:, :, 3 * d:], 1, 2))
    fox_wo = bf(fox_w_out)
    diff_wqkv_t = bf(jnp.swapaxes(diff_w_in, 1, 2))
    diff_wo = bf(diff_w_out)
    conv_w1, conv_w2 = bf(conv_w_pw1), bf(conv_w_pw2)
    tri = (jnp.arange(ATT_T)[:, None] <= jnp.arange(ATT_T)[None, :]).astype(BF16)
    inv_freq = (ROPE_THETA ** (-jnp.arange(0, ROT_DIM, 2, dtype=F32) / ROT_DIM))[:, None]
    pos = positions[:, None, :]

    flat = lambda a: a.reshape(b * s, d)
    i_fox = i_conv = i_diff = 0
    x2 = flat(x)
    for i in range(depth):
        x2 = _ffn(x2, g1, *f1, i)
        x3 = x2.reshape(b, s, d)
        kind = i % N_MIXERS
        if kind == 0:
            q_t, k, v_t = _fox_proj(x3, gm, fox_wqkv_t, fox_wf_t, col(fox_b_f),
                                    col(fox_q_gain), col(fox_k_gain), tri, i, i_fox)
            o = _fox_attn(q_t, k, v_t)
            x2 = _ffn(x2, g2, *f2, i, pre=(flat(o), fox_wo, i_fox))
            i_fox += 1
        elif kind == 1:
            x3 = _conv_module(x3, gm, conv_w1, row(conv_b_pw1), conv_w_dw, row(conv_b_dw),
                              row(conv_ln_g), row(conv_ln_b), conv_w2, row(conv_b_pw2), i, i_conv)
            x2 = _ffn(flat(x3), g2, *f2, i)
            i_conv += 1
        else:
            lambda_init = 0.8 - 0.6 * math.exp(-0.3 * i)
            q_t, k, v_t = _diff_proj(x3, pos, gm, diff_wqkv_t, col(diff_q_gain),
                                     col(diff_k_gain), inv_freq, i, i_diff)
            o = _diff_attn(q_t, k, v_t, row(diff_lambda_q1), row(diff_lambda_k1),
                           row(diff_lambda_q2), row(diff_lambda_k2), col(diff_sub_gain),
                           i_diff, lambda_init)
            x2 = _ffn(x2, g2, *f2, i, pre=(flat(o), diff_wo, i_diff))
            i_diff += 1
    return x2.reshape(b, s, d)
```

```python
import functools
import math

import jax
import jax.numpy as jnp
from jax import lax
from jax.experimental import pallas as pl
from jax.experimental.pallas import tpu as pltpu

D_MODEL = 1024
HEAD_DIM = 64
FOX_HEADS = D_MODEL // HEAD_DIM
DIFF_HEADS = D_MODEL // (2 * HEAD_DIM)
ROT_DIM = HEAD_DIM // 4
ROPE_THETA = 500000.0
CONV_WIDTH = 31
CHUNK = 64
N_MIXERS = 3
EPS = 1e-6
LOG2E = math.log2(math.e)
QSCALE = HEAD_DIM ** -0.5 * LOG2E

VMEM_LIMIT_BYTES = 56 * 1024 * 1024
SUBLANES = 8
SLAB = 128
AUG_ROWS = 16

FFN_TM = 1024
ATT_T = 512
ATT_TK = 256
ATT_TRIP = 4
CONV_TM = 256
CONV_RC = 32
CONV_HALO = 32
FFN_FC = 256

F32 = jnp.float32
BF16 = jnp.bfloat16
NT_DIMS = (((1,), (1,)), ((), ()))


def _rms(x, g):
    return x * lax.rsqrt(jnp.mean(x * x, axis=-1, keepdims=True) + EPS) * g


def _resident(shape):
    nd = len(shape)
    return pl.BlockSpec(shape, lambda *_: (0,) * nd, pipeline_mode=pl.Buffered(1))


def _layer_weight(shape):
    nd = len(shape)

    def make(l):
        return pl.BlockSpec((None,) + tuple(shape), lambda *_: (l,) + (0,) * nd,
                            pipeline_mode=pl.Buffered(1))
    return make


def _params(sem):
    return pltpu.CompilerParams(dimension_semantics=sem, vmem_limit_bytes=VMEM_LIMIT_BYTES)


def _ffn_kernel(*refs, has_pre):
    if has_pre:
        o_ref, wo_ref, x_ref, g_ref, wg_ref, wu_ref, wd_ref, out_ref = refs
        x = x_ref[...] + jnp.dot(o_ref[...], wo_ref[...], preferred_element_type=F32)
    else:
        x_ref, g_ref, wg_ref, wu_ref, wd_ref, out_ref = refs
        x = x_ref[...]
    h = _rms(x, g_ref[...]).astype(BF16)
    d_ff = wg_ref.shape[1]
    y = jnp.zeros_like(x)
    for c in range(d_ff // FFN_FC):
        sl = slice(c * FFN_FC, (c + 1) * FFN_FC)
        gate = jnp.dot(h, wg_ref[:, sl], preferred_element_type=F32)
        up = jnp.dot(h, wu_ref[:, sl], preferred_element_type=F32)
        a = (gate * jax.nn.sigmoid(gate) * up).astype(BF16)
        y = y + jnp.dot(a, wd_ref[sl, :], preferred_element_type=F32)
    out_ref[...] = x + 0.5 * y


def _ffn(x2d, gain, wg, wu, wd, layer, pre=None):
    n, d = x2d.shape
    d_ff = wg.shape[-1]
    row = lambda i: (i, 0)
    in_specs = [pl.BlockSpec((FFN_TM, d), row),
                _layer_weight((1, d))(layer),
                _layer_weight((d, d_ff))(layer),
                _layer_weight((d, d_ff))(layer),
                _layer_weight((d_ff, d))(layer)]
    args = [x2d, gain, wg, wu, wd]
    if pre is not None:
        o2d, wo, wo_layer = pre
        in_specs = [pl.BlockSpec((FFN_TM, d), row), _layer_weight((d, d))(wo_layer)] + in_specs
        args = [o2d, wo] + args
    return pl.pallas_call(
        functools.partial(_ffn_kernel, has_pre=pre is not None),
        out_shape=jax.ShapeDtypeStruct((n, d), F32),
        grid=(n // FFN_TM,),
        in_specs=in_specs,
        out_specs=pl.BlockSpec((FFN_TM, d), row),
        compiler_params=_params(("parallel",)),
        name="ffn_pre" if pre is not None else "ffn",
    )(*args)


def _head_norm_t(xt, gain_col):
    return xt * lax.rsqrt(jnp.mean(xt * xt, axis=0, keepdims=True) + EPS) * gain_col


def _split3(c):
    hi = c.astype(BF16).astype(F32)
    mid = (c - hi).astype(BF16).astype(F32)
    lo = (c - hi - mid).astype(BF16).astype(F32)
    return hi, mid, lo


def _fox_proj_kernel(x_ref, g_ref, wqkv_ref, wf_ref, bf_ref, qg_ref, kg_ref, tri_ref,
                     q_out, k_out, v_out, carry_ref):
    tm = x_ref.shape[0]

    @pl.when(pl.program_id(1) == 0)
    def _():
        carry_ref[...] = jnp.zeros_like(carry_ref)

    h = _rms(x_ref[...], g_ref[...]).astype(BF16)
    qkv = lax.dot_general(wqkv_ref[...], h, NT_DIMS, preferred_element_type=F32)
    for c in range(ATT_T // ATT_TK):
        v_out[c] = qkv[2 * D_MODEL:, c * ATT_TK:(c + 1) * ATT_TK].astype(BF16)
    f = lax.dot_general(wf_ref[...], h, NT_DIMS, preferred_element_type=F32) + bf_ref[...]
    logf = jax.nn.log_sigmoid(f)
    tri = tri_ref[...]
    cum = carry_ref[...]
    for piece in _split3(logf):
        cum = cum + jnp.dot(piece.astype(BF16), tri, preferred_element_type=F32)
    carry_ref[...] = cum[:, tm - 1:tm]

    row = lax.broadcasted_iota(jnp.int32, (AUG_ROWS, tm), 0)
    tail = SLAB - HEAD_DIM - AUG_ROWS
    for hh in range(FOX_HEADS):
        c_hi, c_mid, c_lo = _split3(cum[hh:hh + 1, :] * LOG2E)
        q_aug = jnp.where(row < 3, 1.0,
                          jnp.where(row == 3, c_hi, jnp.where(row == 4, c_mid,
                                                              jnp.where(row == 5, c_lo, 0.0))))
        k_aug = jnp.where(row == 0, -c_hi,
                          jnp.where(row == 1, -c_mid, jnp.where(row == 2, -c_lo,
                                                                jnp.where(row < 6, 1.0, 0.0))))
        qt = qkv[hh * HEAD_DIM:(hh + 1) * HEAD_DIM, :]
        kt = qkv[D_MODEL + hh * HEAD_DIM:D_MODEL + (hh + 1) * HEAD_DIM, :]
        q_out[hh, 0:HEAD_DIM, :] = (_head_norm_t(qt, qg_ref[...]) * QSCALE).astype(BF16)
        q_out[hh, HEAD_DIM:HEAD_DIM + AUG_ROWS, :] = q_aug.astype(BF16)
        q_out[hh, HEAD_DIM + AUG_ROWS:SLAB, :] = jnp.zeros((tail, tm), BF16)
        k_slab = jnp.concatenate([_head_norm_t(kt, kg_ref[...]), k_aug, jnp.zeros((tail, tm), F32)],
                                 axis=0)
        k_out[:, hh * SLAB:(hh + 1) * SLAB] = k_slab.T.astype(BF16)


def _fox_proj(x, gain, wqkv_t, wf_t, b_f, q_gain, k_gain, tri, layer, idx):
    b, s, d = x.shape
    nt = s // ATT_T
    tile = lambda bi, i: (bi, i, 0)
    return pl.pallas_call(
        _fox_proj_kernel,
        out_shape=(jax.ShapeDtypeStruct((b, nt, FOX_HEADS, SLAB, ATT_T), BF16),
                   jax.ShapeDtypeStruct((b, s, FOX_HEADS * SLAB), BF16),
                   jax.ShapeDtypeStruct((b, s // ATT_TK, d, ATT_TK), BF16)),
        grid=(b, nt),
        in_specs=[pl.BlockSpec((None, ATT_T, d), tile),
                  _layer_weight((1, d))(layer),
                  _layer_weight((3 * d, d))(idx),
                  _layer_weight((FOX_HEADS, d))(idx),
                  _layer_weight((FOX_HEADS, 1))(idx),
                  _layer_weight((HEAD_DIM, 1))(idx),
                  _layer_weight((HEAD_DIM, 1))(idx),
                  _resident((ATT_T, ATT_T))],
        out_specs=(pl.BlockSpec((None, None, FOX_HEADS, SLAB, ATT_T), lambda bi, i: (bi, i, 0, 0, 0)),
                   pl.BlockSpec((None, ATT_T, FOX_HEADS * SLAB), tile),
                   pl.BlockSpec((None, ATT_T // ATT_TK, d, ATT_TK), lambda bi, i: (bi, i, 0, 0))),
        scratch_shapes=[pltpu.VMEM((FOX_HEADS, 1), F32)],
        compiler_params=_params(("parallel", "arbitrary")),
        name="fox_proj",
    )(x, gain, wqkv_t, wf_t, b_f, q_gain, k_gain, tri)


def _attend(q_of, k_of, v_of, s_ref, m_ref, l_ref, acc_ref, qi, visible):
    sub = ATT_T // ATT_TK
    assert sub % 2 == 0 and ATT_TRIP % 2 == 0 and ATT_TRIP % sub == 0
    n_full = qi * sub
    m_ref[...] = jnp.full_like(m_ref, -1e30)
    l_ref[...] = jnp.zeros_like(l_ref)
    acc_ref[...] = jnp.zeros_like(acc_ref)

    def scores(slot, ki, c0=0):
        for j in range(2):
            s_ref[slot, j, :, c0:] = jnp.dot(k_of(ki, j), q_of(j, c0),
                                             preferred_element_type=F32)

    def update(slot, ki, mask=None, c0=0):
        for j in range(2):
            s = s_ref[slot, j, :, c0:]
            if mask is not None:
                s = jnp.where(mask, s, -jnp.inf)
            m_prev = m_ref[j, :, c0:]
            m_new = jnp.maximum(m_prev, jnp.max(s, axis=0, keepdims=True))
            alpha = jnp.exp2(m_prev - m_new)
            p = jnp.exp2(s - m_new)
            l_ref[j, :, c0:] = alpha * l_ref[j, :, c0:] + jnp.sum(p, axis=0, keepdims=True)
            acc_ref[j, :, c0:] = alpha * acc_ref[j, :, c0:] + jnp.dot(
                v_of(ki, j), p.astype(BF16), preferred_element_type=F32)
            m_ref[j, :, c0:] = m_new

    def run(first, count):
        for c in range(count):
            scores((c + 1) % 2, first + c + 1)
            update(c % 2, first + c)

    def body(i, carry):
        run(i * ATT_TRIP, ATT_TRIP)
        return carry

    scores(0, 0)
    trips = n_full // ATT_TRIP
    lax.fori_loop(0, trips, body, 0)
    for rem in range(sub, ATT_TRIP, sub):
        @pl.when(n_full - trips * ATT_TRIP == rem)
        def _():
            run(trips * ATT_TRIP, rem)
    for c in range(sub):
        if c + 1 < sub:
            scores((c + 1) % 2, n_full + c + 1, (c + 1) * ATT_TK)
        update(c % 2, n_full + c, visible(c * ATT_TK, c * ATT_TK), c * ATT_TK)


def _attn_scratch(dv):
    return [pltpu.VMEM((2, 2, ATT_TK, ATT_T), F32),
            pltpu.VMEM((2, 1, ATT_T), F32),
            pltpu.VMEM((2, 1, ATT_T), F32),
            pltpu.VMEM((2, dv, ATT_T), F32)]


def _key_rows(ki):
    return pl.ds(pl.multiple_of(ki * ATT_TK, ATT_TK), ATT_TK)


def _key_query_ids(key0, c0):
    shape = (ATT_TK, ATT_T - c0)
    key = key0 + lax.broadcasted_iota(jnp.int32, shape, 0)
    qry = c0 + lax.broadcasted_iota(jnp.int32, shape, 1)
    return key, qry


def _query_rows(qi):
    return pl.ds(pl.multiple_of(qi * ATT_T, ATT_T), ATT_T)


def _fox_attn_kernel(q_ref, k_ref, v_ref, o_ref, s_ref, m_ref, l_ref, acc_ref):
    def causal(key0, c0):
        key, qry = _key_query_ids(key0, c0)
        return key <= qry

    def query_tile(qi, carry):
        _attend(lambda j, c0: q_ref[qi, j, :, c0:],
                lambda ki, j: k_ref[_key_rows(ki), j * SLAB:(j + 1) * SLAB],
                lambda ki, j: v_ref[ki, j * HEAD_DIM:(j + 1) * HEAD_DIM, :],
                s_ref, m_ref, l_ref, acc_ref, qi, causal)
        o_t = jnp.concatenate([acc_ref[0] / l_ref[0], acc_ref[1] / l_ref[1]], axis=0)
        o_ref[_query_rows(qi), :] = o_t.T.astype(BF16)
        return carry

    lax.fori_loop(0, q_ref.shape[0], query_tile, 0)


def _fox_attn(q_t, k, v_t):
    b, nk, d, _ = v_t.shape
    s = nk * ATT_TK
    nt = s // ATT_T
    pair = 2 * HEAD_DIM
    return pl.pallas_call(
        _fox_attn_kernel,
        out_shape=jax.ShapeDtypeStruct((b, s, d), BF16),
        grid=(b, FOX_HEADS // 2),
        in_specs=[pl.BlockSpec((None, nt, 2, SLAB, ATT_T), lambda bi, hp: (bi, 0, hp, 0, 0)),
                  pl.BlockSpec((None, s, 2 * SLAB), lambda bi, hp: (bi, 0, hp)),
                  pl.BlockSpec((None, nk, pair, ATT_TK), lambda bi, hp: (bi, 0, hp, 0))],
        out_specs=pl.BlockSpec((None, s, pair), lambda bi, hp: (bi, 0, hp)),
        scratch_shapes=_attn_scratch(HEAD_DIM),
        compiler_params=_params(("parallel", "parallel")),
        name="fox_attn",
    )(q_t, k, v_t)


def _diff_proj_kernel(x_ref, pos_ref, g_ref, wqkv_ref, qg_ref, kg_ref, invf_ref,
                      q_out, k_out, v_out):
    half = ROT_DIM // 2
    h = _rms(x_ref[...], g_ref[...]).astype(BF16)
    qkv = lax.dot_general(wqkv_ref[...], h, NT_DIMS, preferred_element_type=F32)
    for c in range(ATT_T // ATT_TK):
        v_out[c] = qkv[2 * D_MODEL:, c * ATT_TK:(c + 1) * ATT_TK].astype(BF16)
    ang = invf_ref[...] * pos_ref[...].astype(F32)
    cos = jnp.cos(ang)
    sin = jnp.sin(ang)

    def rope_t(xt):
        x1, x2 = xt[0:half], xt[half:ROT_DIM]
        return jnp.concatenate([x1 * cos - x2 * sin, x2 * cos + x1 * sin, xt[ROT_DIM:]], axis=0)

    def head_t(base, gain_ref):
        maps = [rope_t(_head_norm_t(qkv[base + mp * HEAD_DIM:base + (mp + 1) * HEAD_DIM, :],
                                    gain_ref[...])) for mp in range(2)]
        return jnp.concatenate(maps, axis=0)

    for hh in range(DIFF_HEADS):
        q_out[hh] = (head_t(hh * SLAB, qg_ref) * QSCALE).astype(BF16)
        k_out[:, hh * SLAB:(hh + 1) * SLAB] = head_t(D_MODEL + hh * SLAB, kg_ref).T.astype(BF16)


def _diff_proj(x, pos, gain, wqkv_t, q_gain, k_gain, inv_freq, layer, idx):
    b, s, d = x.shape
    nt = s // ATT_T
    tile = lambda bi, i: (bi, i, 0)
    return pl.pallas_call(
        _diff_proj_kernel,
        out_shape=(jax.ShapeDtypeStruct((b, nt, DIFF_HEADS, SLAB, ATT_T), BF16),
                   jax.ShapeDtypeStruct((b, s, d), BF16),
                   jax.ShapeDtypeStruct((b, s // ATT_TK, d, ATT_TK), BF16)),
        grid=(b, nt),
        in_specs=[pl.BlockSpec((None, ATT_T, d), tile),
                  pl.BlockSpec((None, 1, ATT_T), lambda bi, i: (bi, 0, i)),
                  _layer_weight((1, d))(layer),
                  _layer_weight((3 * d, d))(idx),
                  _layer_weight((HEAD_DIM, 1))(idx),
                  _layer_weight((HEAD_DIM, 1))(idx),
                  _resident((ROT_DIM // 2, 1))],
        out_specs=(pl.BlockSpec((None, None, DIFF_HEADS, SLAB, ATT_T), lambda bi, i: (bi, i, 0, 0, 0)),
                   pl.BlockSpec((None, ATT_T, d), tile),
                   pl.BlockSpec((None, ATT_T // ATT_TK, d, ATT_TK), lambda bi, i: (bi, i, 0, 0))),
        compiler_params=_params(("parallel", "parallel")),
        name="diff_proj",
    )(x, pos, gain, wqkv_t, q_gain, k_gain, inv_freq)


def _diff_attn_kernel(q_ref, k_ref, v_ref, lq1_ref, lk1_ref, lq2_ref, lk2_ref, sg_ref, o_ref,
                      qm_ref, s_ref, m_ref, l_ref, acc_ref, *, lambda_init):
    lam = (jnp.exp(jnp.sum(lq1_ref[...] * lk1_ref[...], axis=1, keepdims=True))
           - jnp.exp(jnp.sum(lq2_ref[...] * lk2_ref[...], axis=1, keepdims=True)) + lambda_init)
    feat = lax.broadcasted_iota(jnp.int32, (SLAB, ATT_T), 0)

    def chunk_causal(key0, c0):
        key, qry = _key_query_ids(key0, c0)
        return (key // CHUNK) <= (qry // CHUNK)

    def query_tile(qi, carry):
        q = q_ref[qi].astype(F32)
        qm_ref[0] = jnp.where(feat < HEAD_DIM, q, 0.0).astype(BF16)
        qm_ref[1] = jnp.where(feat >= HEAD_DIM, q, 0.0).astype(BF16)
        _attend(lambda j, c0: qm_ref[j, :, c0:],
                lambda ki, j: k_ref[_key_rows(ki), :],
                lambda ki, j: v_ref[ki],
                s_ref, m_ref, l_ref, acc_ref, qi, chunk_causal)
        o_t = acc_ref[0] / l_ref[0] - lam * (acc_ref[1] / l_ref[1])
        o_t = o_t * lax.rsqrt(jnp.mean(o_t * o_t, axis=0, keepdims=True) + EPS) * sg_ref[...]
        o_ref[_query_rows(qi), :] = (o_t * (1.0 - lambda_init)).T.astype(BF16)
        return carry

    lax.fori_loop(0, q_ref.shape[0], query_tile, 0)


def _diff_attn(q_t, k, v_t, lq1, lk1, lq2, lk2, sub_gain, idx, lambda_init):
    b, nk, d, _ = v_t.shape
    s = nk * ATT_TK
    nt = s // ATT_T
    vec = _layer_weight((1, HEAD_DIM))(idx)
    return pl.pallas_call(
        functools.partial(_diff_attn_kernel, lambda_init=lambda_init),
        out_shape=jax.ShapeDtypeStruct((b, s, d), BF16),
        grid=(b, DIFF_HEADS),
        in_specs=[pl.BlockSpec((None, nt, None, SLAB, ATT_T), lambda bi, hh: (bi, 0, hh, 0, 0)),
                  pl.BlockSpec((None, s, SLAB), lambda bi, hh: (bi, 0, hh)),
                  pl.BlockSpec((None, nk, SLAB, ATT_TK), lambda bi, hh: (bi, 0, hh, 0)),
                  vec, vec, vec, vec,
                  _layer_weight((2 * HEAD_DIM, 1))(idx)],
        out_specs=pl.BlockSpec((None, s, SLAB), lambda bi, hh: (bi, 0, hh)),
        scratch_shapes=[pltpu.VMEM((2, SLAB, ATT_T), BF16)] + _attn_scratch(2 * HEAD_DIM),
        compiler_params=_params(("parallel", "parallel")),
        name="diff_attn",
    )(q_t, k, v_t, lq1, lk1, lq2, lk2, sub_gain)


def _conv_kernel(x_ref, g_ref, w1_ref, b1_ref, wdw_ref, bdw_ref, lng_ref, lnb_ref, w2_ref, b2_ref,
                 out_ref, u_ref, us_ref, a_ref):
    tm, d = x_ref.shape

    @pl.when(pl.program_id(1) == 0)
    def _():
        u_ref[0:CONV_HALO, :] = jnp.zeros((CONV_HALO, d), F32)

    @pl.when(pl.program_id(1) > 0)
    def _():
        u_ref[0:CONV_HALO, :] = u_ref[tm:tm + CONV_HALO, :]

    x = x_ref[...]
    h = _rms(x, g_ref[...]).astype(BF16)
    u = jnp.dot(h, w1_ref[...], preferred_element_type=F32) + b1_ref[...]
    u_ref[CONV_HALO:CONV_HALO + tm, :] = u[:, :d] * jax.nn.sigmoid(u[:, d:])
    span = us_ref.shape[1]
    for b in range(1, SUBLANES):
        us_ref[b - 1] = u_ref[b:b + span, :]

    first = CONV_HALO - (CONV_WIDTH - 1)
    for c in range(tm // CONV_RC):
        base = c * CONV_RC
        acc = jnp.broadcast_to(bdw_ref[...], (CONV_RC, d))
        for j in range(CONV_WIDTH):
            b = (first + j) % SUBLANES
            lo = base + first + j - b
            rows = u_ref[lo:lo + CONV_RC, :] if b == 0 else us_ref[b - 1, lo:lo + CONV_RC, :]
            acc = acc + wdw_ref[j:j + 1, :] * rows
        mu = jnp.mean(acc, axis=-1, keepdims=True)
        cen = acc - mu
        var = jnp.mean(cen * cen, axis=-1, keepdims=True)
        y = cen * lax.rsqrt(var + EPS) * lng_ref[...] + lnb_ref[...]
        a_ref[base:base + CONV_RC, :] = (y * jax.nn.sigmoid(y)).astype(BF16)

    out_ref[...] = x + jnp.dot(a_ref[...], w2_ref[...], preferred_element_type=F32) + b2_ref[...]


def _conv_module(x, gain, w1, b1, wdw, bdw, lng, lnb, w2, b2, layer, idx):
    b, s, d = x.shape
    tile = pl.BlockSpec((None, CONV_TM, d), lambda bi, i: (bi, i, 0))
    vec = _layer_weight((1, d))(idx)
    return pl.pallas_call(
        _conv_kernel,
        out_shape=jax.ShapeDtypeStruct((b, s, d), F32),
        grid=(b, s // CONV_TM),
        in_specs=[tile,
                  _layer_weight((1, d))(layer),
                  _layer_weight((d, 2 * d))(idx),
                  _layer_weight((1, 2 * d))(idx),
                  _layer_weight((CONV_WIDTH, d))(idx),
                  vec, vec, vec,
                  _layer_weight((d, d))(idx),
                  vec],
        out_specs=tile,
        scratch_shapes=[pltpu.VMEM((CONV_HALO + CONV_TM, d), F32),
                        pltpu.VMEM((SUBLANES - 1, CONV_HALO + CONV_TM - SUBLANES, d), F32),
                        pltpu.VMEM((CONV_TM, d), BF16)],
        compiler_params=_params(("parallel", "arbitrary")),
        name="conv_module",
    )(x, gain, w1, b1, wdw, bdw, lng, lnb, w2, b2)


def kernel(x, positions, norm_ffn1, norm_mix, norm_ffn2, ffn1_w_gate, ffn1_w_up, ffn1_w_down, ffn2_w_gate, ffn2_w_up, ffn2_w_down, fox_w_in, fox_b_f, fox_q_gain, fox_k_gain, fox_w_out, conv_w_pw1, conv_b_pw1, conv_w_dw, conv_b_dw, conv_ln_g, conv_ln_b, conv_w_pw2, conv_b_pw2, diff_w_in, diff_q_gain, diff_k_gain, diff_lambda_q1, diff_lambda_k1, diff_lambda_q2, diff_lambda_k2, diff_sub_gain, diff_w_out):
    b, s, d = x.shape
    depth = norm_ffn1.shape[0]
    assert d == D_MODEL and s % ATT_T == 0 and s % CONV_TM == 0 and (b * s) % FFN_TM == 0

    bf = lambda w: w.astype(BF16)
    row = lambda v: v[:, None, :]
    col = lambda v: v[:, :, None]
    g1, gm, g2 = row(norm_ffn1), row(norm_mix), row(norm_ffn2)
    f1 = (bf(ffn1_w_gate), bf(ffn1_w_up), bf(ffn1_w_down))
    f2 = (bf(ffn2_w_gate), bf(ffn2_w_up), bf(ffn2_w_down))
    fox_wqkv_t = bf(jnp.swapaxes(fox_w_in[:, :, :3 * d], 1, 2))
    fox_wf_t = bf(jnp.swapaxes(fox_w_in[:, :, 3 * d:], 1, 2))
    fox_wo = bf(fox_w_out)
    diff_wqkv_t = bf(jnp.swapaxes(diff_w_in, 1, 2))
    diff_wo = bf(diff_w_out)
    conv_w1, conv_w2 = bf(conv_w_pw1), bf(conv_w_pw2)
    tri = (jnp.arange(ATT_T)[:, None] <= jnp.arange(ATT_T)[None, :]).astype(BF16)
    inv_freq = (ROPE_THETA ** (-jnp.arange(0, ROT_DIM, 2, dtype=F32) / ROT_DIM))[:, None]
    pos = positions[:, None, :]

    flat = lambda a: a.reshape(b * s, d)
    i_fox = i_conv = i_diff = 0
    x2 = flat(x)
    for i in range(depth):
        x2 = _ffn(x2, g1, *f1, i)
        x3 = x2.reshape(b, s, d)
        kind = i % N_MIXERS
        if kind == 0:
            q_t, k, v_t = _fox_proj(x3, gm, fox_wqkv_t, fox_wf_t, col(fox_b_f),
                                    col(fox_q_gain), col(fox_k_gain), tri, i, i_fox)
            o = _fox_attn(q_t, k, v_t)
            x2 = _ffn(x2, g2, *f2, i, pre=(flat(o), fox_wo, i_fox))
            i_fox += 1
        elif kind == 1:
            x3 = _conv_module(x3, gm, conv_w1, row(conv_b_pw1), conv_w_dw, row(conv_b_dw),
                              row(conv_ln_g), row(conv_ln_b), conv_w2, row(conv_b_pw2), i, i_conv)
            x2 = _ffn(flat(x3), g2, *f2, i)
            i_conv += 1
        else:
            lambda_init = 0.8 - 0.6 * math.exp(-0.3 * i)
            q_t, k, v_t = _diff_proj(x3, pos, gm, diff_wqkv_t, col(diff_q_gain),
                                     col(diff_k_gain), inv_freq, i, i_diff)
            o = _diff_attn(q_t, k, v_t, row(diff_lambda_q1), row(diff_lambda_k1),
                           row(diff_lambda_q2), row(diff_lambda_k2), col(diff_sub_gain),
                           i_diff, lambda_init)
            x2 = _ffn(x2, g2, *f2, i, pre=(flat(o), diff_wo, i_diff))
            i_diff += 1
    return x2.reshape(b, s, d)
```

```python
import functools
import math

import jax
import jax.numpy as jnp
from jax import lax
from jax.experimental import pallas as pl
from jax.experimental.pallas import tpu as pltpu

D_MODEL = 1024
HEAD_DIM = 64
FOX_HEADS = D_MODEL // HEAD_DIM
DIFF_HEADS = D_MODEL // (2 * HEAD_DIM)
ROT_DIM = HEAD_DIM // 4
ROPE_THETA = 500000.0
CONV_WIDTH = 31
CHUNK = 64
N_MIXERS = 3
EPS = 1e-6
LOG2E = math.log2(math.e)
QSCALE = HEAD_DIM ** -0.5 * LOG2E

VMEM_LIMIT_BYTES = 56 * 1024 * 1024
SUBLANES = 8
SLAB = 128
AUG_ROWS = 16

FFN_TM = 1024
ATT_T = 512
ATT_TK = 256
ATT_TRIP = 4
CONV_TM = 256
CONV_RC = 32
CONV_HALO = 32
FFN_FC = 256

F32 = jnp.float32
BF16 = jnp.bfloat16
NT_DIMS = (((1,), (1,)), ((), ()))
TN_DIMS = (((0,), (0,)), ((), ()))


def _rms(x, g):
    return x * lax.rsqrt(jnp.mean(x * x, axis=-1, keepdims=True) + EPS) * g


def _resident(shape):
    nd = len(shape)
    return pl.BlockSpec(shape, lambda *_: (0,) * nd, pipeline_mode=pl.Buffered(1))


def _layer_weight(shape):
    nd = len(shape)

    def make(l):
        return pl.BlockSpec((None,) + tuple(shape), lambda *_: (l,) + (0,) * nd,
                            pipeline_mode=pl.Buffered(1))
    return make


def _params(sem):
    return pltpu.CompilerParams(dimension_semantics=sem, vmem_limit_bytes=VMEM_LIMIT_BYTES)


def _ffn_kernel(*refs, has_pre):
    if has_pre:
        o_ref, wo_ref, x_ref, g_ref, wg_ref, wu_ref, wd_ref, out_ref = refs
        proj = [lax.dot_general(o_ref[c], wo_ref[...], TN_DIMS, preferred_element_type=F32)
                for c in range(o_ref.shape[0])]
        x = x_ref[...] + jnp.concatenate(proj, axis=0)
    else:
        x_ref, g_ref, wg_ref, wu_ref, wd_ref, out_ref = refs
        x = x_ref[...]
    h = _rms(x, g_ref[...]).astype(BF16)
    d_ff = wg_ref.shape[1]
    y = jnp.zeros_like(x)
    for c in range(d_ff // FFN_FC):
        sl = slice(c * FFN_FC, (c + 1) * FFN_FC)
        gate = jnp.dot(h, wg_ref[:, sl], preferred_element_type=F32)
        up = jnp.dot(h, wu_ref[:, sl], preferred_element_type=F32)
        a = (gate * jax.nn.sigmoid(gate) * up).astype(BF16)
        y = y + jnp.dot(a, wd_ref[sl, :], preferred_element_type=F32)
    out_ref[...] = x + 0.5 * y


def _ffn(x2d, gain, wg, wu, wd, layer, pre=None):
    n, d = x2d.shape
    d_ff = wg.shape[-1]
    row = lambda i: (i, 0)
    in_specs = [pl.BlockSpec((FFN_TM, d), row),
                _layer_weight((1, d))(layer),
                _layer_weight((d, d_ff))(layer),
                _layer_weight((d, d_ff))(layer),
                _layer_weight((d_ff, d))(layer)]
    args = [x2d, gain, wg, wu, wd]
    if pre is not None:
        o_t, wo, wo_layer = pre
        per_row = o_t.shape[1] * ATT_T // FFN_TM
        o_spec = pl.BlockSpec((None, FFN_TM // ATT_T, d, ATT_T),
                              lambda i: (i // per_row, i % per_row, 0, 0))
        in_specs = [o_spec, _layer_weight((d, d))(wo_layer)] + in_specs
        args = [o_t, wo] + args
    return pl.pallas_call(
        functools.partial(_ffn_kernel, has_pre=pre is not None),
        out_shape=jax.ShapeDtypeStruct((n, d), F32),
        grid=(n // FFN_TM,),
        in_specs=in_specs,
        out_specs=pl.BlockSpec((FFN_TM, d), row),
        compiler_params=_params(("parallel",)),
        name="ffn_pre" if pre is not None else "ffn",
    )(*args)


def _head_norm_t(xt, gain_col):
    return xt * lax.rsqrt(jnp.mean(xt * xt, axis=0, keepdims=True) + EPS) * gain_col


def _split3(c):
    hi = c.astype(BF16).astype(F32)
    mid = (c - hi).astype(BF16).astype(F32)
    lo = (c - hi - mid).astype(BF16).astype(F32)
    return hi, mid, lo


def _fox_proj_kernel(x_ref, g_ref, wqkv_ref, wf_ref, bf_ref, qg_ref, kg_ref, tri_ref,
                     q_out, k_out, v_out, carry_ref):
    tm = x_ref.shape[0]

    @pl.when(pl.program_id(1) == 0)
    def _():
        carry_ref[...] = jnp.zeros_like(carry_ref)

    h = _rms(x_ref[...], g_ref[...]).astype(BF16)
    qkv = lax.dot_general(wqkv_ref[...], h, NT_DIMS, preferred_element_type=F32)
    for c in range(ATT_T // ATT_TK):
        v_out[c] = qkv[2 * D_MODEL:, c * ATT_TK:(c + 1) * ATT_TK].astype(BF16)
    f = lax.dot_general(wf_ref[...], h, NT_DIMS, preferred_element_type=F32) + bf_ref[...]
    logf = jax.nn.log_sigmoid(f)
    tri = tri_ref[...]
    cum = carry_ref[...]
    for piece in _split3(logf):
        cum = cum + jnp.dot(piece.astype(BF16), tri, preferred_element_type=F32)
    carry_ref[...] = cum[:, tm - 1:tm]

    row = lax.broadcasted_iota(jnp.int32, (AUG_ROWS, tm), 0)
    tail = SLAB - HEAD_DIM - AUG_ROWS
    for hh in range(FOX_HEADS):
        c_hi, c_mid, c_lo = _split3(cum[hh:hh + 1, :] * LOG2E)
        q_aug = jnp.where(row < 3, 1.0,
                          jnp.where(row == 3, c_hi, jnp.where(row == 4, c_mid,
                                                              jnp.where(row == 5, c_lo, 0.0))))
        k_aug = jnp.where(row == 0, -c_hi,
                          jnp.where(row == 1, -c_mid, jnp.where(row == 2, -c_lo,
                                                                jnp.where(row < 6, 1.0, 0.0))))
        qt = qkv[hh * HEAD_DIM:(hh + 1) * HEAD_DIM, :]
        kt = qkv[D_MODEL + hh * HEAD_DIM:D_MODEL + (hh + 1) * HEAD_DIM, :]
        q_out[hh, 0:HEAD_DIM, :] = (_head_norm_t(qt, qg_ref[...]) * QSCALE).astype(BF16)
        q_out[hh, HEAD_DIM:HEAD_DIM + AUG_ROWS, :] = q_aug.astype(BF16)
        q_out[hh, HEAD_DIM + AUG_ROWS:SLAB, :] = jnp.zeros((tail, tm), BF16)
        k_slab = jnp.concatenate([_head_norm_t(kt, kg_ref[...]), k_aug, jnp.zeros((tail, tm), F32)],
                                 axis=0)
        k_out[:, hh * SLAB:(hh + 1) * SLAB] = k_slab.T.astype(BF16)


def _fox_proj(x, gain, wqkv_t, wf_t, b_f, q_gain, k_gain, tri, layer, idx):
    b, s, d = x.shape
    nt = s // ATT_T
    tile = lambda bi, i: (bi, i, 0)
    return pl.pallas_call(
        _fox_proj_kernel,
        out_shape=(jax.ShapeDtypeStruct((b, nt, FOX_HEADS, SLAB, ATT_T), BF16),
                   jax.ShapeDtypeStruct((b, s, FOX_HEADS * SLAB), BF16),
                   jax.ShapeDtypeStruct((b, s // ATT_TK, d, ATT_TK), BF16)),
        grid=(b, nt),
        in_specs=[pl.BlockSpec((None, ATT_T, d), tile),
                  _layer_weight((1, d))(layer),
                  _layer_weight((3 * d, d))(idx),
                  _layer_weight((FOX_HEADS, d))(idx),
                  _layer_weight((FOX_HEADS, 1))(idx),
                  _layer_weight((HEAD_DIM, 1))(idx),
                  _layer_weight((HEAD_DIM, 1))(idx),
                  _resident((ATT_T, ATT_T))],
        out_specs=(pl.BlockSpec((None, None, FOX_HEADS, SLAB, ATT_T), lambda bi, i: (bi, i, 0, 0, 0)),
                   pl.BlockSpec((None, ATT_T, FOX_HEADS * SLAB), tile),
                   pl.BlockSpec((None, ATT_T // ATT_TK, d, ATT_TK), lambda bi, i: (bi, i, 0, 0))),
        scratch_shapes=[pltpu.VMEM((FOX_HEADS, 1), F32)],
        compiler_params=_params(("parallel", "arbitrary")),
        name="fox_proj",
    )(x, gain, wqkv_t, wf_t, b_f, q_gain, k_gain, tri)


def _attend(q_of, k_of, v_of, bias_ref, s_ref, m_ref, l_ref, acc_ref, qi):
    sub = ATT_T // ATT_TK
    assert sub % 2 == 0 and ATT_TRIP % 2 == 0 and ATT_TRIP % sub == 0
    n_full = qi * sub
    m_ref[...] = jnp.full_like(m_ref, -1e30)
    l_ref[...] = jnp.zeros_like(l_ref)
    acc_ref[...] = jnp.zeros_like(acc_ref)

    def scores(slot, ki, c0=0):
        for j in range(2):
            s_ref[slot, j, :, c0:] = jnp.dot(k_of(ki, j), q_of(j, c0),
                                             preferred_element_type=F32)

    def update(slot, ki, diag=None, c0=0):
        for j in range(2):
            s = s_ref[slot, j, :, c0:]
            if diag is not None:
                s = s + bias_ref[diag, :, c0:]
            m_prev = m_ref[j, :, c0:]
            m_new = jnp.maximum(m_prev, jnp.max(s, axis=0, keepdims=True))
            alpha = jnp.exp2(m_prev - m_new)
            p = jnp.exp2(s - m_new)
            l_ref[j, :, c0:] = alpha * l_ref[j, :, c0:] + jnp.sum(p, axis=0, keepdims=True)
            acc_ref[j, :, c0:] = alpha * acc_ref[j, :, c0:] + jnp.dot(
                v_of(ki, j), p.astype(BF16), preferred_element_type=F32)
            m_ref[j, :, c0:] = m_new

    def run(first, count):
        for c in range(count):
            scores((c + 1) % 2, first + c + 1)
            update(c % 2, first + c)

    def body(i, carry):
        run(i * ATT_TRIP, ATT_TRIP)
        return carry

    def tail(first, full):
        for c in range(full + sub):
            if c + 1 < full + sub:
                scores((c + 1) % 2, first + c + 1, max(c + 1 - full, 0) * ATT_TK)
            if c < full:
                update(c % 2, first + c)
            else:
                update(c % 2, first + c, c - full, (c - full) * ATT_TK)

    scores(0, 0)
    trips = n_full // ATT_TRIP
    lax.fori_loop(0, trips, body, 0)
    for rem in range(0, ATT_TRIP, sub):
        @pl.when(n_full - trips * ATT_TRIP == rem)
        def _():
            tail(trips * ATT_TRIP, rem)


def _attn_scratch(dv):
    return [pltpu.VMEM((ATT_T // ATT_TK, ATT_TK, ATT_T), F32),
            pltpu.VMEM((2, 2, ATT_TK, ATT_T), F32),
            pltpu.VMEM((2, 1, ATT_T), F32),
            pltpu.VMEM((2, 1, ATT_T), F32),
            pltpu.VMEM((2, dv, ATT_T), F32)]


def _key_rows(ki):
    return pl.ds(pl.multiple_of(ki * ATT_TK, ATT_TK), ATT_TK)


def _fill_diag_bias(bias_ref, visible):
    for d in range(bias_ref.shape[0]):
        key = d * ATT_TK + lax.broadcasted_iota(jnp.int32, (ATT_TK, ATT_T), 0)
        qry = lax.broadcasted_iota(jnp.int32, (ATT_TK, ATT_T), 1)
        bias_ref[d] = jnp.where(visible(key, qry), 0.0, -jnp.inf)


def _fox_attn_kernel(q_ref, k_ref, v_ref, o_ref, bias_ref, s_ref, m_ref, l_ref, acc_ref):
    _fill_diag_bias(bias_ref, lambda key, qry: key <= qry)

    def query_tile(qi, carry):
        _attend(lambda j, c0: q_ref[qi, j, :, c0:],
                lambda ki, j: k_ref[_key_rows(ki), j * SLAB:(j + 1) * SLAB],
                lambda ki, j: v_ref[ki, j * HEAD_DIM:(j + 1) * HEAD_DIM, :],
                bias_ref, s_ref, m_ref, l_ref, acc_ref, qi)
        o_t = jnp.concatenate([acc_ref[0] / l_ref[0], acc_ref[1] / l_ref[1]], axis=0)
        o_ref[qi] = o_t.astype(BF16)
        return carry

    lax.fori_loop(0, q_ref.shape[0], query_tile, 0)


def _fox_attn(q_t, k, v_t):
    b, nk, d, _ = v_t.shape
    s = nk * ATT_TK
    nt = s // ATT_T
    pair = 2 * HEAD_DIM
    return pl.pallas_call(
        _fox_attn_kernel,
        out_shape=jax.ShapeDtypeStruct((b, nt, d, ATT_T), BF16),
        grid=(b, FOX_HEADS // 2),
        in_specs=[pl.BlockSpec((None, nt, 2, SLAB, ATT_T), lambda bi, hp: (bi, 0, hp, 0, 0)),
                  pl.BlockSpec((None, s, 2 * SLAB), lambda bi, hp: (bi, 0, hp)),
                  pl.BlockSpec((None, nk, pair, ATT_TK), lambda bi, hp: (bi, 0, hp, 0))],
        out_specs=pl.BlockSpec((None, nt, pair, ATT_T), lambda bi, hp: (bi, 0, hp, 0)),
        scratch_shapes=_attn_scratch(HEAD_DIM),
        compiler_params=_params(("parallel", "parallel")),
        name="fox_attn",
    )(q_t, k, v_t)


def _diff_proj_kernel(x_ref, pos_ref, g_ref, wqkv_ref, qg_ref, kg_ref, invf_ref,
                      q_out, k_out, v_out):
    half = ROT_DIM // 2
    h = _rms(x_ref[...], g_ref[...]).astype(BF16)
    qkv = lax.dot_general(wqkv_ref[...], h, NT_DIMS, preferred_element_type=F32)
    for c in range(ATT_T // ATT_TK):
        v_out[c] = qkv[2 * D_MODEL:, c * ATT_TK:(c + 1) * ATT_TK].astype(BF16)
    ang = invf_ref[...] * pos_ref[...].astype(F32)
    cos = jnp.cos(ang)
    sin = jnp.sin(ang)

    def rope_t(xt):
        x1, x2 = xt[0:half], xt[half:ROT_DIM]
        return jnp.concatenate([x1 * cos - x2 * sin, x2 * cos + x1 * sin, xt[ROT_DIM:]], axis=0)

    def head_t(base, gain_ref):
        maps = [rope_t(_head_norm_t(qkv[base + mp * HEAD_DIM:base + (mp + 1) * HEAD_DIM, :],
                                    gain_ref[...])) for mp in range(2)]
        return jnp.concatenate(maps, axis=0)

    for hh in range(DIFF_HEADS):
        q_out[hh] = (head_t(hh * SLAB, qg_ref) * QSCALE).astype(BF16)
        k_out[:, hh * SLAB:(hh + 1) * SLAB] = head_t(D_MODEL + hh * SLAB, kg_ref).T.astype(BF16)


def _diff_proj(x, pos, gain, wqkv_t, q_gain, k_gain, inv_freq, layer, idx):
    b, s, d = x.shape
    nt = s // ATT_T
    tile = lambda bi, i: (bi, i, 0)
    return pl.pallas_call(
        _diff_proj_kernel,
        out_shape=(jax.ShapeDtypeStruct((b, nt, DIFF_HEADS, SLAB, ATT_T), BF16),
                   jax.ShapeDtypeStruct((b, s, d), BF16),
                   jax.ShapeDtypeStruct((b, s // ATT_TK, d, ATT_TK), BF16)),
        grid=(b, nt),
        in_specs=[pl.BlockSpec((None, ATT_T, d), tile),
                  pl.BlockSpec((None, 1, ATT_T), lambda bi, i: (bi, 0, i)),
                  _layer_weight((1, d))(layer),
                  _layer_weight((3 * d, d))(idx),
                  _layer_weight((HEAD_DIM, 1))(idx),
                  _layer_weight((HEAD_DIM, 1))(idx),
                  _resident((ROT_DIM // 2, 1))],
        out_specs=(pl.BlockSpec((None, None, DIFF_HEADS, SLAB, ATT_T), lambda bi, i: (bi, i, 0, 0, 0)),
                   pl.BlockSpec((None, ATT_T, d), tile),
                   pl.BlockSpec((None, ATT_T // ATT_TK, d, ATT_TK), lambda bi, i: (bi, i, 0, 0))),
        compiler_params=_params(("parallel", "parallel")),
        name="diff_proj",
    )(x, pos, gain, wqkv_t, q_gain, k_gain, inv_freq)


def _diff_attn_kernel(q_ref, k_ref, v_ref, lq1_ref, lk1_ref, lq2_ref, lk2_ref, sg_ref, o_ref,
                      qm_ref, bias_ref, s_ref, m_ref, l_ref, acc_ref, *, lambda_init):
    lam = (jnp.exp(jnp.sum(lq1_ref[...] * lk1_ref[...], axis=1, keepdims=True))
           - jnp.exp(jnp.sum(lq2_ref[...] * lk2_ref[...], axis=1, keepdims=True)) + lambda_init)
    feat = lax.broadcasted_iota(jnp.int32, (SLAB, ATT_T), 0)
    _fill_diag_bias(bias_ref, lambda key, qry: (key // CHUNK) <= (qry // CHUNK))

    def query_tile(qi, carry):
        q = q_ref[qi].astype(F32)
        qm_ref[0] = jnp.where(feat < HEAD_DIM, q, 0.0).astype(BF16)
        qm_ref[1] = jnp.where(feat >= HEAD_DIM, q, 0.0).astype(BF16)
        _attend(lambda j, c0: qm_ref[j, :, c0:],
                lambda ki, j: k_ref[_key_rows(ki), :],
                lambda ki, j: v_ref[ki],
                bias_ref, s_ref, m_ref, l_ref, acc_ref, qi)
        o_t = acc_ref[0] / l_ref[0] - lam * (acc_ref[1] / l_ref[1])
        o_t = o_t * lax.rsqrt(jnp.mean(o_t * o_t, axis=0, keepdims=True) + EPS) * sg_ref[...]
        o_ref[qi] = (o_t * (1.0 - lambda_init)).astype(BF16)
        return carry

    lax.fori_loop(0, q_ref.shape[0], query_tile, 0)


def _diff_attn(q_t, k, v_t, lq1, lk1, lq2, lk2, sub_gain, idx, lambda_init):
    b, nk, d, _ = v_t.shape
    s = nk * ATT_TK
    nt = s // ATT_T
    vec = _layer_weight((1, HEAD_DIM))(idx)
    return pl.pallas_call(
        functools.partial(_diff_attn_kernel, lambda_init=lambda_init),
        out_shape=jax.ShapeDtypeStruct((b, nt, d, ATT_T), BF16),
        grid=(b, DIFF_HEADS),
        in_specs=[pl.BlockSpec((None, nt, None, SLAB, ATT_T), lambda bi, hh: (bi, 0, hh, 0, 0)),
                  pl.BlockSpec((None, s, SLAB), lambda bi, hh: (bi, 0, hh)),
                  pl.BlockSpec((None, nk, SLAB, ATT_TK), lambda bi, hh: (bi, 0, hh, 0)),
                  vec, vec, vec, vec,
                  _layer_weight((2 * HEAD_DIM, 1))(idx)],
        out_specs=pl.BlockSpec((None, nt, SLAB, ATT_T), lambda bi, hh: (bi, 0, hh, 0)),
        scratch_shapes=[pltpu.VMEM((2, SLAB, ATT_T), BF16)] + _attn_scratch(2 * HEAD_DIM),
        compiler_params=_params(("parallel", "parallel")),
        name="diff_attn",
    )(q_t, k, v_t, lq1, lk1, lq2, lk2, sub_gain)


def _conv_kernel(x_ref, g_ref, w1_ref, b1_ref, wdw_ref, bdw_ref, lng_ref, lnb_ref, w2_ref, b2_ref,
                 out_ref, u_ref, us_ref, a_ref):
    tm, d = x_ref.shape

    @pl.when(pl.program_id(1) == 0)
    def _():
        u_ref[0:CONV_HALO, :] = jnp.zeros((CONV_HALO, d), F32)

    @pl.when(pl.program_id(1) > 0)
    def _():
        u_ref[0:CONV_HALO, :] = u_ref[tm:tm + CONV_HALO, :]

    x = x_ref[...]
    h = _rms(x, g_ref[...]).astype(BF16)
    u = jnp.dot(h, w1_ref[...], preferred_element_type=F32) + b1_ref[...]
    u_ref[CONV_HALO:CONV_HALO + tm, :] = u[:, :d] * jax.nn.sigmoid(u[:, d:])
    span = us_ref.shape[1]
    for b in range(1, SUBLANES):
        us_ref[b - 1] = u_ref[b:b + span, :]

    first = CONV_HALO - (CONV_WIDTH - 1)
    for c in range(tm // CONV_RC):
        base = c * CONV_RC
        acc = jnp.broadcast_to(bdw_ref[...], (CONV_RC, d))
        for j in range(CONV_WIDTH):
            b = (first + j) % SUBLANES
            lo = base + first + j - b
            rows = u_ref[lo:lo + CONV_RC, :] if b == 0 else us_ref[b - 1, lo:lo + CONV_RC, :]
            acc = acc + wdw_ref[j:j + 1, :] * rows
        mu = jnp.mean(acc, axis=-1, keepdims=True)
        cen = acc - mu
        var = jnp.mean(cen * cen, axis=-1, keepdims=True)
        y = cen * lax.rsqrt(var + EPS) * lng_ref[...] + lnb_ref[...]
        a_ref[base:base + CONV_RC, :] = (y * jax.nn.sigmoid(y)).astype(BF16)

    out_ref[...] = x + jnp.dot(a_ref[...], w2_ref[...], preferred_element_type=F32) + b2_ref[...]


def _conv_module(x, gain, w1, b1, wdw, bdw, lng, lnb, w2, b2, layer, idx):
    b, s, d = x.shape
    tile = pl.BlockSpec((None, CONV_TM, d), lambda bi, i: (bi, i, 0))
    vec = _layer_weight((1, d))(idx)
    return pl.pallas_call(
        _conv_kernel,
        out_shape=jax.ShapeDtypeStruct((b, s, d), F32),
        grid=(b, s // CONV_TM),
        in_specs=[tile,
                  _layer_weight((1, d))(layer),
                  _layer_weight((d, 2 * d))(idx),
                  _layer_weight((1, 2 * d))(idx),
                  _layer_weight((CONV_WIDTH, d))(idx),
                  vec, vec, vec,
                  _layer_weight((d, d))(idx),
                  vec],
        out_specs=tile,
        scratch_shapes=[pltpu.VMEM((CONV_HALO + CONV_TM, d), F32),
                        pltpu.VMEM((SUBLANES - 1, CONV_HALO + CONV_TM - SUBLANES, d), F32),
                        pltpu.VMEM((CONV_TM, d), BF16)],
        compiler_params=_params(("parallel", "arbitrary")),
        name="conv_module",
    )(x, gain, w1, b1, wdw, bdw, lng, lnb, w2, b2)


def kernel(x, positions, norm_ffn1, norm_mix, norm_ffn2, ffn1_w_gate, ffn1_w_up, ffn1_w_down, ffn2_w_gate, ffn2_w_up, ffn2_w_down, fox_w_in, fox_b_f, fox_q_gain, fox_k_gain, fox_w_out, conv_w_pw1, conv_b_pw1, conv_w_dw, conv_b_dw, conv_ln_g, conv_ln_b, conv_w_pw2, conv_b_pw2, diff_w_in, diff_q_gain, diff_k_gain, diff_lambda_q1, diff_lambda_k1, diff_lambda_q2, diff_lambda_k2, diff_sub_gain, diff_w_out):
    b, s, d = x.shape
    depth = norm_ffn1.shape[0]
    assert d == D_MODEL and s % ATT_T == 0 and s % CONV_TM == 0 and (b * s) % FFN_TM == 0

    bf = lambda w: w.astype(BF16)
    row = lambda v: v[:, None, :]
    col = lambda v: v[:, :, None]
    g1, gm, g2 = row(norm_ffn1), row(norm_mix), row(norm_ffn2)
    f1 = (bf(ffn1_w_gate), bf(ffn1_w_up), bf(ffn1_w_down))
    f2 = (bf(ffn2_w_gate), bf(ffn2_w_up), bf(ffn2_w_down))
    fox_wqkv_t = bf(jnp.swapaxes(fox_w_in[:, :, :3 * d], 1, 2))
    fox_wf_t = bf(jnp.swapaxes(fox_w_in[:, :, 3 * d:], 1, 2))
    fox_wo = bf(fox_w_out)
    diff_wqkv_t = bf(jnp.swapaxes(diff_w_in, 1, 2))
    diff_wo = bf(diff_w_out)
    conv_w1, conv_w2 = bf(conv_w_pw1), bf(conv_w_pw2)
    tri = (jnp.arange(ATT_T)[:, None] <= jnp.arange(ATT_T)[None, :]).astype(BF16)
    inv_freq = (ROPE_THETA ** (-jnp.arange(0, ROT_DIM, 2, dtype=F32) / ROT_DIM))[:, None]
    pos = positions[:, None, :]

    flat = lambda a: a.reshape(b * s, d)
    i_fox = i_conv = i_diff = 0
    x2 = flat(x)
    for i in range(depth):
        x2 = _ffn(x2, g1, *f1, i)
        x3 = x2.reshape(b, s, d)
        kind = i % N_MIXERS
        if kind == 0:
            q_t, k, v_t = _fox_proj(x3, gm, fox_wqkv_t, fox_wf_t, col(fox_b_f),
                                    col(fox_q_gain), col(fox_k_gain), tri, i, i_fox)
            o = _fox_attn(q_t, k, v_t)
            x2 = _ffn(x2, g2, *f2, i, pre=(o, fox_wo, i_fox))
            i_fox += 1
        elif kind == 1:
            x3 = _conv_module(x3, gm, conv_w1, row(conv_b_pw1), conv_w_dw, row(conv_b_dw),
                              row(conv_ln_g), row(conv_ln_b), conv_w2, row(conv_b_pw2), i, i_conv)
            x2 = _ffn(flat(x3), g2, *f2, i)
            i_conv += 1
        else:
            lambda_init = 0.8 - 0.6 * math.exp(-0.3 * i)
            q_t, k, v_t = _diff_proj(x3, pos, gm, diff_wqkv_t, col(diff_q_gain),
                                     col(diff_k_gain), inv_freq, i, i_diff)
            o = _diff_attn(q_t, k, v_t, row(diff_lambda_q1), row(diff_lambda_k1),
                           row(diff_lambda_q2), row(diff_lambda_k2), col(diff_sub_gain),
                           i_diff, lambda_init)
            x2 = _ffn(x2, g2, *f2, i, pre=(o, diff_wo, i_diff))
            i_diff += 1
    return x2.reshape(b, s, d)
```

```python
import functools
import math

import jax
import jax.numpy as jnp
from jax import lax
from jax.experimental import pallas as pl
from jax.experimental.pallas import tpu as pltpu

D_MODEL = 1024
HEAD_DIM = 64
FOX_HEADS = D_MODEL // HEAD_DIM
DIFF_HEADS = D_MODEL // (2 * HEAD_DIM)
ROT_DIM = HEAD_DIM // 4
ROPE_THETA = 500000.0
CONV_WIDTH = 31
CHUNK = 64
N_MIXERS = 3
EPS = 1e-6
LOG2E = math.log2(math.e)
QSCALE = HEAD_DIM ** -0.5 * LOG2E

VMEM_LIMIT_BYTES = 56 * 1024 * 1024
SUBLANES = 8
SLAB = 128
AUG_ROWS = 16

FFN_TM = 1024
ATT_T = 512
ATT_TK = 256
ATT_TRIP = 4
CONV_TM = 256
CONV_RC = 32
CONV_HALO = 32
FFN_FC = 256

F32 = jnp.float32
BF16 = jnp.bfloat16
NT_DIMS = (((1,), (1,)), ((), ()))
TN_DIMS = (((0,), (0,)), ((), ()))


def _rms(x, g):
    return x * lax.rsqrt(jnp.mean(x * x, axis=-1, keepdims=True) + EPS) * g


def _resident(shape):
    nd = len(shape)
    return pl.BlockSpec(shape, lambda *_: (0,) * nd, pipeline_mode=pl.Buffered(1))


def _layer_weight(shape):
    nd = len(shape)

    def make(l):
        return pl.BlockSpec((None,) + tuple(shape), lambda *_: (l,) + (0,) * nd,
                            pipeline_mode=pl.Buffered(1))
    return make


def _params(sem):
    return pltpu.CompilerParams(dimension_semantics=sem, vmem_limit_bytes=VMEM_LIMIT_BYTES)


def _ffn_kernel(*refs, has_pre):
    if has_pre:
        o_ref, wo_ref, x_ref, g_ref, wg_ref, wu_ref, wd_ref, out_ref = refs
        proj = [lax.dot_general(o_ref[c], wo_ref[...], TN_DIMS, preferred_element_type=F32)
                for c in range(o_ref.shape[0])]
        x = x_ref[...] + jnp.concatenate(proj, axis=0)
    else:
        x_ref, g_ref, wg_ref, wu_ref, wd_ref, out_ref = refs
        x = x_ref[...]
    h = _rms(x, g_ref[...]).astype(BF16)
    d_ff = wg_ref.shape[1]
    y = jnp.zeros_like(x)
    for c in range(d_ff // FFN_FC):
        sl = slice(c * FFN_FC, (c + 1) * FFN_FC)
        gate = jnp.dot(h, wg_ref[:, sl], preferred_element_type=F32)
        up = jnp.dot(h, wu_ref[:, sl], preferred_element_type=F32)
        a = (gate * jax.nn.sigmoid(gate) * up).astype(BF16)
        y = y + jnp.dot(a, wd_ref[sl, :], preferred_element_type=F32)
    out_ref[...] = x + 0.5 * y


def _ffn(x2d, gain, wg, wu, wd, layer, pre=None):
    n, d = x2d.shape
    d_ff = wg.shape[-1]
    row = lambda i: (i, 0)
    in_specs = [pl.BlockSpec((FFN_TM, d), row),
                _layer_weight((1, d))(layer),
                _layer_weight((d, d_ff))(layer),
                _layer_weight((d, d_ff))(layer),
                _layer_weight((d_ff, d))(layer)]
    args = [x2d, gain, wg, wu, wd]
    if pre is not None:
        o_t, wo, wo_layer = pre
        per_row = o_t.shape[1] * ATT_T // FFN_TM
        o_spec = pl.BlockSpec((None, FFN_TM // ATT_T, d, ATT_T),
                              lambda i: (i // per_row, i % per_row, 0, 0))
        in_specs = [o_spec, _layer_weight((d, d))(wo_layer)] + in_specs
        args = [o_t, wo] + args
    return pl.pallas_call(
        functools.partial(_ffn_kernel, has_pre=pre is not None),
        out_shape=jax.ShapeDtypeStruct((n, d), F32),
        grid=(n // FFN_TM,),
        in_specs=in_specs,
        out_specs=pl.BlockSpec((FFN_TM, d), row),
        compiler_params=_params(("parallel",)),
        name="ffn_pre" if pre is not None else "ffn",
    )(*args)


def _head_norm_t(xt, gain_col):
    return xt * lax.rsqrt(jnp.mean(xt * xt, axis=0, keepdims=True) + EPS) * gain_col


def _split3(c):
    hi = c.astype(BF16).astype(F32)
    mid = (c - hi).astype(BF16).astype(F32)
    lo = (c - hi - mid).astype(BF16).astype(F32)
    return hi, mid, lo


def _fox_proj_kernel(x_ref, g_ref, wqkv_ref, wf_ref, bf_ref, qg_ref, kg_ref, tri_ref,
                     q_out, k_out, v_out, carry_ref):
    tm = x_ref.shape[0]

    @pl.when(pl.program_id(1) == 0)
    def _():
        carry_ref[...] = jnp.zeros_like(carry_ref)

    h = _rms(x_ref[...], g_ref[...]).astype(BF16)
    qkv = lax.dot_general(wqkv_ref[...], h, NT_DIMS, preferred_element_type=F32)
    for c in range(ATT_T // ATT_TK):
        v_out[c] = qkv[2 * D_MODEL:, c * ATT_TK:(c + 1) * ATT_TK].astype(BF16)
    f = lax.dot_general(wf_ref[...], h, NT_DIMS, preferred_element_type=F32) + bf_ref[...]
    logf = jax.nn.log_sigmoid(f)
    tri = tri_ref[...]
    cum = carry_ref[...]
    for piece in _split3(logf):
        cum = cum + jnp.dot(piece.astype(BF16), tri, preferred_element_type=F32)
    carry_ref[...] = cum[:, tm - 1:tm]

    row = lax.broadcasted_iota(jnp.int32, (AUG_ROWS, tm), 0)
    tail = SLAB - HEAD_DIM - AUG_ROWS
    for hh in range(FOX_HEADS):
        c_hi, c_mid, c_lo = _split3(cum[hh:hh + 1, :] * LOG2E)
        q_aug = jnp.where(row < 3, 1.0,
                          jnp.where(row == 3, c_hi, jnp.where(row == 4, c_mid,
                                                              jnp.where(row == 5, c_lo, 0.0))))
        k_aug = jnp.where(row == 0, -c_hi,
                          jnp.where(row == 1, -c_mid, jnp.where(row == 2, -c_lo,
                                                                jnp.where(row < 6, 1.0, 0.0))))
        qt = qkv[hh * HEAD_DIM:(hh + 1) * HEAD_DIM, :]
        kt = qkv[D_MODEL + hh * HEAD_DIM:D_MODEL + (hh + 1) * HEAD_DIM, :]
        q_out[hh, 0:HEAD_DIM, :] = (_head_norm_t(qt, qg_ref[...]) * QSCALE).astype(BF16)
        q_out[hh, HEAD_DIM:HEAD_DIM + AUG_ROWS, :] = q_aug.astype(BF16)
        q_out[hh, HEAD_DIM + AUG_ROWS:SLAB, :] = jnp.zeros((tail, tm), BF16)
        k_slab = jnp.concatenate([_head_norm_t(kt, kg_ref[...]), k_aug, jnp.zeros((tail, tm), F32)],
                                 axis=0)
        k_out[:, hh * SLAB:(hh + 1) * SLAB] = k_slab.T.astype(BF16)


def _fox_proj(x, gain, wqkv_t, wf_t, b_f, q_gain, k_gain, tri, layer, idx):
    b, s, d = x.shape
    nt = s // ATT_T
    tile = lambda bi, i: (bi, i, 0)
    return pl.pallas_call(
        _fox_proj_kernel,
        out_shape=(jax.ShapeDtypeStruct((b, nt, FOX_HEADS, SLAB, ATT_T), BF16),
                   jax.ShapeDtypeStruct((b, s, FOX_HEADS * SLAB), BF16),
                   jax.ShapeDtypeStruct((b, s // ATT_TK, d, ATT_TK), BF16)),
        grid=(b, nt),
        in_specs=[pl.BlockSpec((None, ATT_T, d), tile),
                  _layer_weight((1, d))(layer),
                  _layer_weight((3 * d, d))(idx),
                  _layer_weight((FOX_HEADS, d))(idx),
                  _layer_weight((FOX_HEADS, 1))(idx),
                  _layer_weight((HEAD_DIM, 1))(idx),
                  _layer_weight((HEAD_DIM, 1))(idx),
                  _resident((ATT_T, ATT_T))],
        out_specs=(pl.BlockSpec((None, None, FOX_HEADS, SLAB, ATT_T), lambda bi, i: (bi, i, 0, 0, 0)),
                   pl.BlockSpec((None, ATT_T, FOX_HEADS * SLAB), tile),
                   pl.BlockSpec((None, ATT_T // ATT_TK, d, ATT_TK), lambda bi, i: (bi, i, 0, 0))),
        scratch_shapes=[pltpu.VMEM((FOX_HEADS, 1), F32)],
        compiler_params=_params(("parallel", "arbitrary")),
        name="fox_proj",
    )(x, gain, wqkv_t, wf_t, b_f, q_gain, k_gain, tri)


def _attend(q_of, k_of, v_of, bias_ref, s_ref, smax_ref, m_ref, l_ref, acc_ref, qi):
    sub = ATT_T // ATT_TK
    assert sub % 2 == 0 and ATT_TRIP % 2 == 0 and ATT_TRIP % sub == 0
    n_full = qi * sub
    m_ref[...] = jnp.full_like(m_ref, -1e30)
    l_ref[...] = jnp.zeros_like(l_ref)
    acc_ref[...] = jnp.zeros_like(acc_ref)

    def scores(slot, ki, c0=0):
        for j in range(2):
            s = jnp.dot(k_of(ki, j), q_of(j, c0), preferred_element_type=F32)
            s_ref[slot, j, :, c0:] = s
            smax_ref[slot, j, :, c0:] = jnp.max(s, axis=0, keepdims=True)

    def update(slot, ki, diag=None, c0=0):
        for j in range(2):
            s = s_ref[slot, j, :, c0:]
            if diag is None:
                s_max = smax_ref[slot, j, :, c0:]
            else:
                s = s + bias_ref[diag, :, c0:]
                s_max = jnp.max(s, axis=0, keepdims=True)
            m_prev = m_ref[j, :, c0:]
            m_new = jnp.maximum(m_prev, s_max)
            alpha = jnp.exp2(m_prev - m_new)
            p = jnp.exp2(s - m_new)
            l_ref[j, :, c0:] = alpha * l_ref[j, :, c0:] + jnp.sum(p, axis=0, keepdims=True)
            acc_ref[j, :, c0:] = alpha * acc_ref[j, :, c0:] + jnp.dot(
                v_of(ki, j), p.astype(BF16), preferred_element_type=F32)
            m_ref[j, :, c0:] = m_new

    def run(first, count):
        for c in range(count):
            scores((c + 1) % 2, first + c + 1)
            update(c % 2, first + c)

    def body(i, carry):
        run(i * ATT_TRIP, ATT_TRIP)
        return carry

    def tail(first, full):
        for c in range(full + sub):
            if c + 1 < full + sub:
                scores((c + 1) % 2, first + c + 1, max(c + 1 - full, 0) * ATT_TK)
            if c < full:
                update(c % 2, first + c)
            else:
                update(c % 2, first + c, c - full, (c - full) * ATT_TK)

    scores(0, 0)
    trips = n_full // ATT_TRIP
    lax.fori_loop(0, trips, body, 0)
    for rem in range(0, ATT_TRIP, sub):
        @pl.when(n_full - trips * ATT_TRIP == rem)
        def _():
            tail(trips * ATT_TRIP, rem)


def _attn_scratch(dv):
    return [pltpu.VMEM((ATT_T // ATT_TK, ATT_TK, ATT_T), F32),
            pltpu.VMEM((2, 2, ATT_TK, ATT_T), F32),
            pltpu.VMEM((2, 2, 1, ATT_T), F32),
            pltpu.VMEM((2, 1, ATT_T), F32),
            pltpu.VMEM((2, 1, ATT_T), F32),
            pltpu.VMEM((2, dv, ATT_T), F32)]


def _key_rows(ki):
    return pl.ds(pl.multiple_of(ki * ATT_TK, ATT_TK), ATT_TK)


def _fill_diag_bias(bias_ref, visible):
    for d in range(bias_ref.shape[0]):
        key = d * ATT_TK + lax.broadcasted_iota(jnp.int32, (ATT_TK, ATT_T), 0)
        qry = lax.broadcasted_iota(jnp.int32, (ATT_TK, ATT_T), 1)
        bias_ref[d] = jnp.where(visible(key, qry), 0.0, -jnp.inf)


def _fox_attn_kernel(q_ref, k_ref, v_ref, o_ref, bias_ref, s_ref, smax_ref, m_ref, l_ref, acc_ref):
    _fill_diag_bias(bias_ref, lambda key, qry: key <= qry)

    def query_tile(qi, carry):
        _attend(lambda j, c0: q_ref[qi, j, :, c0:],
                lambda ki, j: k_ref[_key_rows(ki), j * SLAB:(j + 1) * SLAB],
                lambda ki, j: v_ref[ki, j * HEAD_DIM:(j + 1) * HEAD_DIM, :],
                bias_ref, s_ref, smax_ref, m_ref, l_ref, acc_ref, qi)
        o_t = jnp.concatenate([acc_ref[0] / l_ref[0], acc_ref[1] / l_ref[1]], axis=0)
        o_ref[qi] = o_t.astype(BF16)
        return carry

    lax.fori_loop(0, q_ref.shape[0], query_tile, 0)


def _fox_attn(q_t, k, v_t):
    b, nk, d, _ = v_t.shape
    s = nk * ATT_TK
    nt = s // ATT_T
    pair = 2 * HEAD_DIM
    return pl.pallas_call(
        _fox_attn_kernel,
        out_shape=jax.ShapeDtypeStruct((b, nt, d, ATT_T), BF16),
        grid=(b, FOX_HEADS // 2),
        in_specs=[pl.BlockSpec((None, nt, 2, SLAB, ATT_T), lambda bi, hp: (bi, 0, hp, 0, 0)),
                  pl.BlockSpec((None, s, 2 * SLAB), lambda bi, hp: (bi, 0, hp)),
                  pl.BlockSpec((None, nk, pair, ATT_TK), lambda bi, hp: (bi, 0, hp, 0))],
        out_specs=pl.BlockSpec((None, nt, pair, ATT_T), lambda bi, hp: (bi, 0, hp, 0)),
        scratch_shapes=_attn_scratch(HEAD_DIM),
        compiler_params=_params(("parallel", "parallel")),
        name="fox_attn",
    )(q_t, k, v_t)


def _diff_proj_kernel(x_ref, pos_ref, g_ref, wqkv_ref, qg_ref, kg_ref, invf_ref,
                      q_out, k_out, v_out):
    half = ROT_DIM // 2
    h = _rms(x_ref[...], g_ref[...]).astype(BF16)
    qkv = lax.dot_general(wqkv_ref[...], h, NT_DIMS, preferred_element_type=F32)
    for c in range(ATT_T // ATT_TK):
        v_out[c] = qkv[2 * D_MODEL:, c * ATT_TK:(c + 1) * ATT_TK].astype(BF16)
    ang = invf_ref[...] * pos_ref[...].astype(F32)
    cos = jnp.cos(ang)
    sin = jnp.sin(ang)

    def rope_t(xt):
        x1, x2 = xt[0:half], xt[half:ROT_DIM]
        return jnp.concatenate([x1 * cos - x2 * sin, x2 * cos + x1 * sin, xt[ROT_DIM:]], axis=0)

    def head_t(base, gain_ref):
        maps = [rope_t(_head_norm_t(qkv[base + mp * HEAD_DIM:base + (mp + 1) * HEAD_DIM, :],
                                    gain_ref[...])) for mp in range(2)]
        return jnp.concatenate(maps, axis=0)

    for hh in range(DIFF_HEADS):
        q_out[hh] = (head_t(hh * SLAB, qg_ref) * QSCALE).astype(BF16)
        k_out[:, hh * SLAB:(hh + 1) * SLAB] = head_t(D_MODEL + hh * SLAB, kg_ref).T.astype(BF16)


def _diff_proj(x, pos, gain, wqkv_t, q_gain, k_gain, inv_freq, layer, idx):
    b, s, d = x.shape
    nt = s // ATT_T
    tile = lambda bi, i: (bi, i, 0)
    return pl.pallas_call(
        _diff_proj_kernel,
        out_shape=(jax.ShapeDtypeStruct((b, nt, DIFF_HEADS, SLAB, ATT_T), BF16),
                   jax.ShapeDtypeStruct((b, s, d), BF16),
                   jax.ShapeDtypeStruct((b, s // ATT_TK, d, ATT_TK), BF16)),
        grid=(b, nt),
        in_specs=[pl.BlockSpec((None, ATT_T, d), tile),
                  pl.BlockSpec((None, 1, ATT_T), lambda bi, i: (bi, 0, i)),
                  _layer_weight((1, d))(layer),
                  _layer_weight((3 * d, d))(idx),
                  _layer_weight((HEAD_DIM, 1))(idx),
                  _layer_weight((HEAD_DIM, 1))(idx),
                  _resident((ROT_DIM // 2, 1))],
        out_specs=(pl.BlockSpec((None, None, DIFF_HEADS, SLAB, ATT_T), lambda bi, i: (bi, i, 0, 0, 0)),
                   pl.BlockSpec((None, ATT_T, d), tile),
                   pl.BlockSpec((None, ATT_T // ATT_TK, d, ATT_TK), lambda bi, i: (bi, i, 0, 0))),
        compiler_params=_params(("parallel", "parallel")),
        name="diff_proj",
    )(x, pos, gain, wqkv_t, q_gain, k_gain, inv_freq)


def _diff_attn_kernel(q_ref, k_ref, v_ref, lq1_ref, lk1_ref, lq2_ref, lk2_ref, sg_ref, o_ref,
                      qm_ref, bias_ref, s_ref, smax_ref, m_ref, l_ref, acc_ref, *, lambda_init):
    lam = (jnp.exp(jnp.sum(lq1_ref[...] * lk1_ref[...], axis=1, keepdims=True))
           - jnp.exp(jnp.sum(lq2_ref[...] * lk2_ref[...], axis=1, keepdims=True)) + lambda_init)
    feat = lax.broadcasted_iota(jnp.int32, (SLAB, ATT_T), 0)
    _fill_diag_bias(bias_ref, lambda key, qry: (key // CHUNK) <= (qry // CHUNK))

    def query_tile(qi, carry):
        q = q_ref[qi].astype(F32)
        qm_ref[0] = jnp.where(feat < HEAD_DIM, q, 0.0).astype(BF16)
        qm_ref[1] = jnp.where(feat >= HEAD_DIM, q, 0.0).astype(BF16)
        _attend(lambda j, c0: qm_ref[j, :, c0:],
                lambda ki, j: k_ref[_key_rows(ki), :],
                lambda ki, j: v_ref[ki],
                bias_ref, s_ref, smax_ref, m_ref, l_ref, acc_ref, qi)
        o_t = acc_ref[0] / l_ref[0] - lam * (acc_ref[1] / l_ref[1])
        o_t = o_t * lax.rsqrt(jnp.mean(o_t * o_t, axis=0, keepdims=True) + EPS) * sg_ref[...]
        o_ref[qi] = (o_t * (1.0 - lambda_init)).astype(BF16)
        return carry

    lax.fori_loop(0, q_ref.shape[0], query_tile, 0)


def _diff_attn(q_t, k, v_t, lq1, lk1, lq2, lk2, sub_gain, idx, lambda_init):
    b, nk, d, _ = v_t.shape
    s = nk * ATT_TK
    nt = s // ATT_T
    vec = _layer_weight((1, HEAD_DIM))(idx)
    return pl.pallas_call(
        functools.partial(_diff_attn_kernel, lambda_init=lambda_init),
        out_shape=jax.ShapeDtypeStruct((b, nt, d, ATT_T), BF16),
        grid=(b, DIFF_HEADS),
        in_specs=[pl.BlockSpec((None, nt, None, SLAB, ATT_T), lambda bi, hh: (bi, 0, hh, 0, 0)),
                  pl.BlockSpec((None, s, SLAB), lambda bi, hh: (bi, 0, hh)),
                  pl.BlockSpec((None, nk, SLAB, ATT_TK), lambda bi, hh: (bi, 0, hh, 0)),
                  vec, vec, vec, vec,
                  _layer_weight((2 * HEAD_DIM, 1))(idx)],
        out_specs=pl.BlockSpec((None, nt, SLAB, ATT_T), lambda bi, hh: (bi, 0, hh, 0)),
        scratch_shapes=[pltpu.VMEM((2, SLAB, ATT_T), BF16)] + _attn_scratch(2 * HEAD_DIM),
        compiler_params=_params(("parallel", "parallel")),
        name="diff_attn",
    )(q_t, k, v_t, lq1, lk1, lq2, lk2, sub_gain)


def _conv_kernel(x_ref, g_ref, w1_ref, b1_ref, wdw_ref, bdw_ref, lng_ref, lnb_ref, w2_ref, b2_ref,
                 out_ref, u_ref, us_ref, a_ref):
    tm, d = x_ref.shape

    @pl.when(pl.program_id(1) == 0)
    def _():
        u_ref[0:CONV_HALO, :] = jnp.zeros((CONV_HALO, d), F32)

    @pl.when(pl.program_id(1) > 0)
    def _():
        u_ref[0:CONV_HALO, :] = u_ref[tm:tm + CONV_HALO, :]

    x = x_ref[...]
    h = _rms(x, g_ref[...]).astype(BF16)
    u = jnp.dot(h, w1_ref[...], preferred_element_type=F32) + b1_ref[...]
    u_ref[CONV_HALO:CONV_HALO + tm, :] = u[:, :d] * jax.nn.sigmoid(u[:, d:])
    span = us_ref.shape[1]
    for b in range(1, SUBLANES):
        us_ref[b - 1] = u_ref[b:b + span, :]

    first = CONV_HALO - (CONV_WIDTH - 1)
    for c in range(tm // CONV_RC):
        base = c * CONV_RC
        acc = jnp.broadcast_to(bdw_ref[...], (CONV_RC, d))
        for j in range(CONV_WIDTH):
            b = (first + j) % SUBLANES
            lo = base + first + j - b
            rows = u_ref[lo:lo + CONV_RC, :] if b == 0 else us_ref[b - 1, lo:lo + CONV_RC, :]
            acc = acc + wdw_ref[j:j + 1, :] * rows
        mu = jnp.mean(acc, axis=-1, keepdims=True)
        cen = acc - mu
        var = jnp.mean(cen * cen, axis=-1, keepdims=True)
        y = cen * lax.rsqrt(var + EPS) * lng_ref[...] + lnb_ref[...]
        a_ref[base:base + CONV_RC, :] = (y * jax.nn.sigmoid(y)).astype(BF16)

    out_ref[...] = x + jnp.dot(a_ref[...], w2_ref[...], preferred_element_type=F32) + b2_ref[...]


def _conv_module(x, gain, w1, b1, wdw, bdw, lng, lnb, w2, b2, layer, idx):
    b, s, d = x.shape
    tile = pl.BlockSpec((None, CONV_TM, d), lambda bi, i: (bi, i, 0))
    vec = _layer_weight((1, d))(idx)
    return pl.pallas_call(
        _conv_kernel,
        out_shape=jax.ShapeDtypeStruct((b, s, d), F32),
        grid=(b, s // CONV_TM),
        in_specs=[tile,
                  _layer_weight((1, d))(layer),
                  _layer_weight((d, 2 * d))(idx),
                  _layer_weight((1, 2 * d))(idx),
                  _layer_weight((CONV_WIDTH, d))(idx),
                  vec, vec, vec,
                  _layer_weight((d, d))(idx),
                  vec],
        out_specs=tile,
        scratch_shapes=[pltpu.VMEM((CONV_HALO + CONV_TM, d), F32),
                        pltpu.VMEM((SUBLANES - 1, CONV_HALO + CONV_TM - SUBLANES, d), F32),
                        pltpu.VMEM((CONV_TM, d), BF16)],
        compiler_params=_params(("parallel", "arbitrary")),
        name="conv_module",
    )(x, gain, w1, b1, wdw, bdw, lng, lnb, w2, b2)


def kernel(x, positions, norm_ffn1, norm_mix, norm_ffn2, ffn1_w_gate, ffn1_w_up, ffn1_w_down, ffn2_w_gate, ffn2_w_up, ffn2_w_down, fox_w_in, fox_b_f, fox_q_gain, fox_k_gain, fox_w_out, conv_w_pw1, conv_b_pw1, conv_w_dw, conv_b_dw, conv_ln_g, conv_ln_b, conv_w_pw2, conv_b_pw2, diff_w_in, diff_q_gain, diff_k_gain, diff_lambda_q1, diff_lambda_k1, diff_lambda_q2, diff_lambda_k2, diff_sub_gain, diff_w_out):
    b, s, d = x.shape
    depth = norm_ffn1.shape[0]
    assert d == D_MODEL and s % ATT_T == 0 and s % CONV_TM == 0 and (b * s) % FFN_TM == 0

    bf = lambda w: w.astype(BF16)
    row = lambda v: v[:, None, :]
    col = lambda v: v[:, :, None]
    g1, gm, g2 = row(norm_ffn1), row(norm_mix), row(norm_ffn2)
    f1 = (bf(ffn1_w_gate), bf(ffn1_w_up), bf(ffn1_w_down))
    f2 = (bf(ffn2_w_gate), bf(ffn2_w_up), bf(ffn2_w_down))
    fox_wqkv_t = bf(jnp.swapaxes(fox_w_in[:, :, :3 * d], 1, 2))
    fox_wf_t = bf(jnp.swapaxes(fox_w_in[:, :, 3 * d:], 1, 2))
    fox_wo = bf(fox_w_out)
    diff_wqkv_t = bf(jnp.swapaxes(diff_w_in, 1, 2))
    diff_wo = bf(diff_w_out)
    conv_w1, conv_w2 = bf(conv_w_pw1), bf(conv_w_pw2)
    tri = (jnp.arange(ATT_T)[:, None] <= jnp.arange(ATT_T)[None, :]).astype(BF16)
    inv_freq = (ROPE_THETA ** (-jnp.arange(0, ROT_DIM, 2, dtype=F32) / ROT_DIM))[:, None]
    pos = positions[:, None, :]

    flat = lambda a: a.reshape(b * s, d)
    i_fox = i_conv = i_diff = 0
    x2 = flat(x)
    for i in range(depth):
        x2 = _ffn(x2, g1, *f1, i)
        x3 = x2.reshape(b, s, d)
        kind = i % N_MIXERS
        if kind == 0:
            q_t, k, v_t = _fox_proj(x3, gm, fox_wqkv_t, fox_wf_t, col(fox_b_f),
                                    col(fox_q_gain), col(fox_k_gain), tri, i, i_fox)
            o = _fox_attn(q_t, k, v_t)
            x2 = _ffn(x2, g2, *f2, i, pre=(o, fox_wo, i_fox))
            i_fox += 1
        elif kind == 1:
            x3 = _conv_module(x3, gm, conv_w1, row(conv_b_pw1), conv_w_dw, row(conv_b_dw),
                              row(conv_ln_g), row(conv_ln_b), conv_w2, row(conv_b_pw2), i, i_conv)
            x2 = _ffn(flat(x3), g2, *f2, i)
            i_conv += 1
        else:
            lambda_init = 0.8 - 0.6 * math.exp(-0.3 * i)
            q_t, k, v_t = _diff_proj(x3, pos, gm, diff_wqkv_t, col(diff_q_gain),
                                     col(diff_k_gain), inv_freq, i, i_diff)
            o = _diff_attn(q_t, k, v_t, row(diff_lambda_q1), row(diff_lambda_k1),
                           row(diff_lambda_q2), row(diff_lambda_k2), col(diff_sub_gain),
                           i_diff, lambda_init)
            x2 = _ffn(x2, g2, *f2, i, pre=(o, diff_wo, i_diff))
            i_diff += 1
    return x2.reshape(b, s, d)
```

```python
import functools
import math

import jax
import jax.numpy as jnp
from jax import lax
from jax.experimental import pallas as pl
from jax.experimental.pallas import tpu as pltpu

D_MODEL = 1024
HEAD_DIM = 64
FOX_HEADS = D_MODEL // HEAD_DIM
DIFF_HEADS = D_MODEL // (2 * HEAD_DIM)
ROT_DIM = HEAD_DIM // 4
ROPE_THETA = 500000.0
CONV_WIDTH = 31
CHUNK = 64
N_MIXERS = 3
EPS = 1e-6
LOG2E = math.log2(math.e)
QSCALE = HEAD_DIM ** -0.5 * LOG2E

VMEM_LIMIT_BYTES = 56 * 1024 * 1024
SUBLANES = 8
SLAB = 128
AUG_ROWS = 16

FFN_TM = 1024
ATT_T = 512
ATT_TK = 256
ATT_TRIP = 8
CONV_TM = 256
CONV_RC = 32
CONV_HALO = 32
FFN_FC = 256

F32 = jnp.float32
BF16 = jnp.bfloat16
NT_DIMS = (((1,), (1,)), ((), ()))
TN_DIMS = (((0,), (0,)), ((), ()))


def _rms(x, g):
    return x * lax.rsqrt(jnp.mean(x * x, axis=-1, keepdims=True) + EPS) * g


def _resident(shape):
    nd = len(shape)
    return pl.BlockSpec(shape, lambda *_: (0,) * nd, pipeline_mode=pl.Buffered(1))


def _layer_weight(shape):
    nd = len(shape)

    def make(l):
        return pl.BlockSpec((None,) + tuple(shape), lambda *_: (l,) + (0,) * nd,
                            pipeline_mode=pl.Buffered(1))
    return make


def _params(sem):
    return pltpu.CompilerParams(dimension_semantics=sem, vmem_limit_bytes=VMEM_LIMIT_BYTES)


def _ffn_kernel(*refs, has_pre):
    if has_pre:
        o_ref, wo_ref, x_ref, g_ref, wg_ref, wu_ref, wd_ref, out_ref = refs
        proj = [lax.dot_general(o_ref[c], wo_ref[...], TN_DIMS, preferred_element_type=F32)
                for c in range(o_ref.shape[0])]
        x = x_ref[...] + jnp.concatenate(proj, axis=0)
    else:
        x_ref, g_ref, wg_ref, wu_ref, wd_ref, out_ref = refs
        x = x_ref[...]
    h = _rms(x, g_ref[...]).astype(BF16)
    d_ff = wg_ref.shape[1]
    y = jnp.zeros_like(x)
    for c in range(d_ff // FFN_FC):
        sl = slice(c * FFN_FC, (c + 1) * FFN_FC)
        gate = jnp.dot(h, wg_ref[:, sl], preferred_element_type=F32)
        up = jnp.dot(h, wu_ref[:, sl], preferred_element_type=F32)
        a = (gate * jax.nn.sigmoid(gate) * up).astype(BF16)
        y = y + jnp.dot(a, wd_ref[sl, :], preferred_element_type=F32)
    out_ref[...] = x + 0.5 * y


def _ffn(x2d, gain, wg, wu, wd, layer, pre=None):
    n, d = x2d.shape
    d_ff = wg.shape[-1]
    row = lambda i: (i, 0)
    in_specs = [pl.BlockSpec((FFN_TM, d), row),
                _layer_weight((1, d))(layer),
                _layer_weight((d, d_ff))(layer),
                _layer_weight((d, d_ff))(layer),
                _layer_weight((d_ff, d))(layer)]
    args = [x2d, gain, wg, wu, wd]
    if pre is not None:
        o_t, wo, wo_layer = pre
        per_row = o_t.shape[1] * ATT_T // FFN_TM
        o_spec = pl.BlockSpec((None, FFN_TM // ATT_T, d, ATT_T),
                              lambda i: (i // per_row, i % per_row, 0, 0))
        in_specs = [o_spec, _layer_weight((d, d))(wo_layer)] + in_specs
        args = [o_t, wo] + args
    return pl.pallas_call(
        functools.partial(_ffn_kernel, has_pre=pre is not None),
        out_shape=jax.ShapeDtypeStruct((n, d), F32),
        grid=(n // FFN_TM,),
        in_specs=in_specs,
        out_specs=pl.BlockSpec((FFN_TM, d), row),
        compiler_params=_params(("parallel",)),
        name="ffn_pre" if pre is not None else "ffn",
    )(*args)


def _head_norm_t(xt, gain_col):
    return xt * lax.rsqrt(jnp.mean(xt * xt, axis=0, keepdims=True) + EPS) * gain_col


def _split3(c):
    hi = c.astype(BF16).astype(F32)
    mid = (c - hi).astype(BF16).astype(F32)
    lo = (c - hi - mid).astype(BF16).astype(F32)
    return hi, mid, lo


def _fox_proj_kernel(x_ref, g_ref, wqkv_ref, wf_ref, bf_ref, qg_ref, kg_ref, tri_ref,
                     q_out, k_out, v_out, carry_ref):
    tm = x_ref.shape[0]

    @pl.when(pl.program_id(1) == 0)
    def _():
        carry_ref[...] = jnp.zeros_like(carry_ref)

    h = _rms(x_ref[...], g_ref[...]).astype(BF16)

    def project(part):
        return lax.dot_general(wqkv_ref[part * D_MODEL:(part + 1) * D_MODEL, :], h, NT_DIMS,
                               preferred_element_type=F32)

    f = lax.dot_general(wf_ref[...], h, NT_DIMS, preferred_element_type=F32) + bf_ref[...]
    kt_all = project(1)
    logf = jax.nn.log_sigmoid(f)
    pieces = jnp.concatenate(_split3(logf), axis=0).astype(BF16)
    part_sums = jnp.dot(pieces, tri_ref[...], preferred_element_type=F32)
    cum = carry_ref[...]
    for i in range(3):
        cum = cum + part_sums[i * FOX_HEADS:(i + 1) * FOX_HEADS, :]
    carry_ref[...] = cum[:, tm - 1:tm]

    row = lax.broadcasted_iota(jnp.int32, (AUG_ROWS, tm), 0)
    tail = SLAB - HEAD_DIM - AUG_ROWS
    bias_pieces = [_split3(cum[hh:hh + 1, :] * LOG2E) for hh in range(FOX_HEADS)]

    for hh in range(FOX_HEADS):
        c_hi, c_mid, c_lo = bias_pieces[hh]
        k_aug = jnp.where(row == 0, -c_hi,
                          jnp.where(row == 1, -c_mid, jnp.where(row == 2, -c_lo,
                                                                jnp.where(row < 6, 1.0, 0.0))))
        kt = kt_all[hh * HEAD_DIM:(hh + 1) * HEAD_DIM, :]
        k_slab = jnp.concatenate([_head_norm_t(kt, kg_ref[...]), k_aug, jnp.zeros((tail, tm), F32)],
                                 axis=0)
        k_out[:, hh * SLAB:(hh + 1) * SLAB] = k_slab.T.astype(BF16)

    qt_all = project(0)
    for hh in range(FOX_HEADS):
        c_hi, c_mid, c_lo = bias_pieces[hh]
        q_aug = jnp.where(row < 3, 1.0,
                          jnp.where(row == 3, c_hi, jnp.where(row == 4, c_mid,
                                                              jnp.where(row == 5, c_lo, 0.0))))
        qt = qt_all[hh * HEAD_DIM:(hh + 1) * HEAD_DIM, :]
        q_out[hh, 0:HEAD_DIM, :] = (_head_norm_t(qt, qg_ref[...]) * QSCALE).astype(BF16)
        q_out[hh, HEAD_DIM:HEAD_DIM + AUG_ROWS, :] = q_aug.astype(BF16)
        q_out[hh, HEAD_DIM + AUG_ROWS:SLAB, :] = jnp.zeros((tail, tm), BF16)

    vt_all = project(2)
    for c in range(ATT_T // ATT_TK):
        v_out[c] = vt_all[:, c * ATT_TK:(c + 1) * ATT_TK].astype(BF16)


def _fox_proj(x, gain, wqkv_t, wf_t, b_f, q_gain, k_gain, tri, layer, idx):
    b, s, d = x.shape
    nt = s // ATT_T
    tile = lambda bi, i: (bi, i, 0)
    return pl.pallas_call(
        _fox_proj_kernel,
        out_shape=(jax.ShapeDtypeStruct((b, nt, FOX_HEADS, SLAB, ATT_T), BF16),
                   jax.ShapeDtypeStruct((b, s, FOX_HEADS * SLAB), BF16),
                   jax.ShapeDtypeStruct((b, s // ATT_TK, d, ATT_TK), BF16)),
        grid=(b, nt),
        in_specs=[pl.BlockSpec((None, ATT_T, d), tile),
                  _layer_weight((1, d))(layer),
                  _layer_weight((3 * d, d))(idx),
                  _layer_weight((FOX_HEADS, d))(idx),
                  _layer_weight((FOX_HEADS, 1))(idx),
                  _layer_weight((HEAD_DIM, 1))(idx),
                  _layer_weight((HEAD_DIM, 1))(idx),
                  _resident((ATT_T, ATT_T))],
        out_specs=(pl.BlockSpec((None, None, FOX_HEADS, SLAB, ATT_T), lambda bi, i: (bi, i, 0, 0, 0)),
                   pl.BlockSpec((None, ATT_T, FOX_HEADS * SLAB), tile),
                   pl.BlockSpec((None, ATT_T // ATT_TK, d, ATT_TK), lambda bi, i: (bi, i, 0, 0))),
        scratch_shapes=[pltpu.VMEM((FOX_HEADS, 1), F32)],
        compiler_params=_params(("parallel", "arbitrary")),
        name="fox_proj",
    )(x, gain, wqkv_t, wf_t, b_f, q_gain, k_gain, tri)


def _attend(q_of, k_of, v_of, bias_ref, s_ref, smax_ref, m_ref, l_ref, acc_ref, qi):
    sub = ATT_T // ATT_TK
    assert sub % 2 == 0 and ATT_TRIP % 2 == 0 and ATT_TRIP % sub == 0
    n_full = qi * sub
    m_ref[...] = jnp.full_like(m_ref, -1e30)
    l_ref[...] = jnp.zeros_like(l_ref)
    acc_ref[...] = jnp.zeros_like(acc_ref)

    def scores(slot, ki, c0=0):
        for j in range(2):
            s = jnp.dot(k_of(ki, j), q_of(j, c0), preferred_element_type=F32)
            s_ref[slot, j, :, c0:] = s
            smax_ref[slot, j, :, c0:] = jnp.max(s, axis=0, keepdims=True)

    def update(slot, ki, diag=None, c0=0):
        for j in range(2):
            s = s_ref[slot, j, :, c0:]
            if diag is None:
                s_max = smax_ref[slot, j, :, c0:]
            else:
                s = s + bias_ref[diag, :, c0:]
                s_max = jnp.max(s, axis=0, keepdims=True)
            m_prev = m_ref[j, :, c0:]
            m_new = jnp.maximum(m_prev, s_max)
            alpha = jnp.exp2(m_prev - m_new)
            p = jnp.exp2(s - m_new)
            l_ref[j, :, c0:] = alpha * l_ref[j, :, c0:] + jnp.sum(p, axis=0, keepdims=True)
            acc_ref[j, :, c0:] = alpha * acc_ref[j, :, c0:] + jnp.dot(
                v_of(ki, j), p.astype(BF16), preferred_element_type=F32)
            m_ref[j, :, c0:] = m_new

    def run(first, count):
        for c in range(count):
            scores((c + 1) % 2, first + c + 1)
            update(c % 2, first + c)

    def body(i, carry):
        run(i * ATT_TRIP, ATT_TRIP)
        return carry

    def tail(first, full):
        for c in range(full + sub):
            if c + 1 < full + sub:
                scores((c + 1) % 2, first + c + 1, max(c + 1 - full, 0) * ATT_TK)
            if c < full:
                update(c % 2, first + c)
            else:
                update(c % 2, first + c, c - full, (c - full) * ATT_TK)

    scores(0, 0)
    trips = n_full // ATT_TRIP
    lax.fori_loop(0, trips, body, 0)
    for rem in range(0, ATT_TRIP, sub):
        @pl.when(n_full - trips * ATT_TRIP == rem)
        def _():
            tail(trips * ATT_TRIP, rem)


def _attn_scratch(dv):
    return [pltpu.VMEM((ATT_T // ATT_TK, ATT_TK, ATT_T), F32),
            pltpu.VMEM((2, 2, ATT_TK, ATT_T), F32),
            pltpu.VMEM((2, 2, 1, ATT_T), F32),
            pltpu.VMEM((2, 1, ATT_T), F32),
            pltpu.VMEM((2, 1, ATT_T), F32),
            pltpu.VMEM((2, dv, ATT_T), F32)]


def _key_rows(ki):
    return pl.ds(pl.multiple_of(ki * ATT_TK, ATT_TK), ATT_TK)


def _fill_diag_bias(bias_ref, visible):
    for d in range(bias_ref.shape[0]):
        key = d * ATT_TK + lax.broadcasted_iota(jnp.int32, (ATT_TK, ATT_T), 0)
        qry = lax.broadcasted_iota(jnp.int32, (ATT_TK, ATT_T), 1)
        bias_ref[d] = jnp.where(visible(key, qry), 0.0, -jnp.inf)


def _fox_attn_kernel(q_ref, k_ref, v_ref, o_ref, bias_ref, s_ref, smax_ref, m_ref, l_ref, acc_ref):
    _fill_diag_bias(bias_ref, lambda key, qry: key <= qry)

    def query_tile(qi, carry):
        _attend(lambda j, c0: q_ref[qi, j, :, c0:],
                lambda ki, j: k_ref[_key_rows(ki), j * SLAB:(j + 1) * SLAB],
                lambda ki, j: v_ref[ki, j * HEAD_DIM:(j + 1) * HEAD_DIM, :],
                bias_ref, s_ref, smax_ref, m_ref, l_ref, acc_ref, qi)
        o_t = jnp.concatenate([acc_ref[0] / l_ref[0], acc_ref[1] / l_ref[1]], axis=0)
        o_ref[qi] = o_t.astype(BF16)
        return carry

    lax.fori_loop(0, q_ref.shape[0], query_tile, 0)


def _fox_attn(q_t, k, v_t):
    b, nk, d, _ = v_t.shape
    s = nk * ATT_TK
    nt = s // ATT_T
    pair = 2 * HEAD_DIM
    return pl.pallas_call(
        _fox_attn_kernel,
        out_shape=jax.ShapeDtypeStruct((b, nt, d, ATT_T), BF16),
        grid=(b, FOX_HEADS // 2),
        in_specs=[pl.BlockSpec((None, nt, 2, SLAB, ATT_T), lambda bi, hp: (bi, 0, hp, 0, 0)),
                  pl.BlockSpec((None, s, 2 * SLAB), lambda bi, hp: (bi, 0, hp)),
                  pl.BlockSpec((None, nk, pair, ATT_TK), lambda bi, hp: (bi, 0, hp, 0))],
        out_specs=pl.BlockSpec((None, nt, pair, ATT_T), lambda bi, hp: (bi, 0, hp, 0)),
        scratch_shapes=_attn_scratch(HEAD_DIM),
        compiler_params=_params(("parallel", "parallel")),
        name="fox_attn",
    )(q_t, k, v_t)


def _diff_proj_kernel(x_ref, pos_ref, g_ref, wqkv_ref, qg_ref, kg_ref, invf_ref,
                      q_out, k_out, v_out):
    half = ROT_DIM // 2
    h = _rms(x_ref[...], g_ref[...]).astype(BF16)
    ang = invf_ref[...] * pos_ref[...].astype(F32)
    cos = jnp.cos(ang)
    sin = jnp.sin(ang)

    def project(part):
        return lax.dot_general(wqkv_ref[part * D_MODEL:(part + 1) * D_MODEL, :], h, NT_DIMS,
                               preferred_element_type=F32)

    def rope_t(xt):
        x1, x2 = xt[0:half], xt[half:ROT_DIM]
        return jnp.concatenate([x1 * cos - x2 * sin, x2 * cos + x1 * sin, xt[ROT_DIM:]], axis=0)

    def head_t(xt_all, hh, gain_ref):
        maps = [rope_t(_head_norm_t(xt_all[hh * SLAB + mp * HEAD_DIM:hh * SLAB + (mp + 1) * HEAD_DIM, :],
                                    gain_ref[...])) for mp in range(2)]
        return jnp.concatenate(maps, axis=0)

    kt_all = project(1)
    for hh in range(DIFF_HEADS):
        k_out[:, hh * SLAB:(hh + 1) * SLAB] = head_t(kt_all, hh, kg_ref).T.astype(BF16)
    qt_all = project(0)
    for hh in range(DIFF_HEADS):
        q_out[hh] = (head_t(qt_all, hh, qg_ref) * QSCALE).astype(BF16)
    vt_all = project(2)
    for c in range(ATT_T // ATT_TK):
        v_out[c] = vt_all[:, c * ATT_TK:(c + 1) * ATT_TK].astype(BF16)


def _diff_proj(x, pos, gain, wqkv_t, q_gain, k_gain, inv_freq, layer, idx):
    b, s, d = x.shape
    nt = s // ATT_T
    tile = lambda bi, i: (bi, i, 0)
    return pl.pallas_call(
        _diff_proj_kernel,
        out_shape=(jax.ShapeDtypeStruct((b, nt, DIFF_HEADS, SLAB, ATT_T), BF16),
                   jax.ShapeDtypeStruct((b, s, d), BF16),
                   jax.ShapeDtypeStruct((b, s // ATT_TK, d, ATT_TK), BF16)),
        grid=(b, nt),
        in_specs=[pl.BlockSpec((None, ATT_T, d), tile),
                  pl.BlockSpec((None, 1, ATT_T), lambda bi, i: (bi, 0, i)),
                  _layer_weight((1, d))(layer),
                  _layer_weight((3 * d, d))(idx),
                  _layer_weight((HEAD_DIM, 1))(idx),
                  _layer_weight((HEAD_DIM, 1))(idx),
                  _resident((ROT_DIM // 2, 1))],
        out_specs=(pl.BlockSpec((None, None, DIFF_HEADS, SLAB, ATT_T), lambda bi, i: (bi, i, 0, 0, 0)),
                   pl.BlockSpec((None, ATT_T, d), tile),
                   pl.BlockSpec((None, ATT_T // ATT_TK, d, ATT_TK), lambda bi, i: (bi, i, 0, 0))),
        compiler_params=_params(("parallel", "parallel")),
        name="diff_proj",
    )(x, pos, gain, wqkv_t, q_gain, k_gain, inv_freq)


def _diff_attn_kernel(q_ref, k_ref, v_ref, lq1_ref, lk1_ref, lq2_ref, lk2_ref, sg_ref, o_ref,
                      qm_ref, bias_ref, s_ref, smax_ref, m_ref, l_ref, acc_ref, *, lambda_init):
    lam = (jnp.exp(jnp.sum(lq1_ref[...] * lk1_ref[...], axis=1, keepdims=True))
           - jnp.exp(jnp.sum(lq2_ref[...] * lk2_ref[...], axis=1, keepdims=True)) + lambda_init)
    feat = lax.broadcasted_iota(jnp.int32, (SLAB, ATT_T), 0)
    _fill_diag_bias(bias_ref, lambda key, qry: (key // CHUNK) <= (qry // CHUNK))

    def query_tile(qi, carry):
        q = q_ref[qi].astype(F32)
        qm_ref[0] = jnp.where(feat < HEAD_DIM, q, 0.0).astype(BF16)
        qm_ref[1] = jnp.where(feat >= HEAD_DIM, q, 0.0).astype(BF16)
        _attend(lambda j, c0: qm_ref[j, :, c0:],
                lambda ki, j: k_ref[_key_rows(ki), :],
                lambda ki, j: v_ref[ki],
                bias_ref, s_ref, smax_ref, m_ref, l_ref, acc_ref, qi)
        o_t = acc_ref[0] / l_ref[0] - lam * (acc_ref[1] / l_ref[1])
        o_t = o_t * lax.rsqrt(jnp.mean(o_t * o_t, axis=0, keepdims=True) + EPS) * sg_ref[...]
        o_ref[qi] = (o_t * (1.0 - lambda_init)).astype(BF16)
        return carry

    lax.fori_loop(0, q_ref.shape[0], query_tile, 0)


def _diff_attn(q_t, k, v_t, lq1, lk1, lq2, lk2, sub_gain, idx, lambda_init):
    b, nk, d, _ = v_t.shape
    s = nk * ATT_TK
    nt = s // ATT_T
    vec = _layer_weight((1, HEAD_DIM))(idx)
    return pl.pallas_call(
        functools.partial(_diff_attn_kernel, lambda_init=lambda_init),
        out_shape=jax.ShapeDtypeStruct((b, nt, d, ATT_T), BF16),
        grid=(b, DIFF_HEADS),
        in_specs=[pl.BlockSpec((None, nt, None, SLAB, ATT_T), lambda bi, hh: (bi, 0, hh, 0, 0)),
                  pl.BlockSpec((None, s, SLAB), lambda bi, hh: (bi, 0, hh)),
                  pl.BlockSpec((None, nk, SLAB, ATT_TK), lambda bi, hh: (bi, 0, hh, 0)),
                  vec, vec, vec, vec,
                  _layer_weight((2 * HEAD_DIM, 1))(idx)],
        out_specs=pl.BlockSpec((None, nt, SLAB, ATT_T), lambda bi, hh: (bi, 0, hh, 0)),
        scratch_shapes=[pltpu.VMEM((2, SLAB, ATT_T), BF16)] + _attn_scratch(2 * HEAD_DIM),
        compiler_params=_params(("parallel", "parallel")),
        name="diff_attn",
    )(q_t, k, v_t, lq1, lk1, lq2, lk2, sub_gain)


def _conv_kernel(x_ref, g_ref, w1_ref, b1_ref, wdw_ref, bdw_ref, lng_ref, lnb_ref, w2_ref, b2_ref,
                 out_ref, u_ref, us_ref, a_ref):
    tm, d = x_ref.shape

    @pl.when(pl.program_id(1) == 0)
    def _():
        u_ref[0:CONV_HALO, :] = jnp.zeros((CONV_HALO, d), F32)

    @pl.when(pl.program_id(1) > 0)
    def _():
        u_ref[0:CONV_HALO, :] = u_ref[tm:tm + CONV_HALO, :]

    x = x_ref[...]
    h = _rms(x, g_ref[...]).astype(BF16)
    u = jnp.dot(h, w1_ref[...], preferred_element_type=F32) + b1_ref[...]
    u_ref[CONV_HALO:CONV_HALO + tm, :] = u[:, :d] * jax.nn.sigmoid(u[:, d:])
    span = us_ref.shape[1]
    for b in range(1, SUBLANES):
        us_ref[b - 1] = u_ref[b:b + span, :]

    first = CONV_HALO - (CONV_WIDTH - 1)
    for c in range(tm // CONV_RC):
        base = c * CONV_RC
        acc = jnp.broadcast_to(bdw_ref[...], (CONV_RC, d))
        for j in range(CONV_WIDTH):
            b = (first + j) % SUBLANES
            lo = base + first + j - b
            rows = u_ref[lo:lo + CONV_RC, :] if b == 0 else us_ref[b - 1, lo:lo + CONV_RC, :]
            acc = acc + wdw_ref[j:j + 1, :] * rows
        mu = jnp.mean(acc, axis=-1, keepdims=True)
        cen = acc - mu
        var = jnp.mean(cen * cen, axis=-1, keepdims=True)
        y = cen * lax.rsqrt(var + EPS) * lng_ref[...] + lnb_ref[...]
        a_ref[base:base + CONV_RC, :] = (y * jax.nn.sigmoid(y)).astype(BF16)

    out_ref[...] = x + jnp.dot(a_ref[...], w2_ref[...], preferred_element_type=F32) + b2_ref[...]


def _conv_module(x, gain, w1, b1, wdw, bdw, lng, lnb, w2, b2, layer, idx):
    b, s, d = x.shape
    tile = pl.BlockSpec((None, CONV_TM, d), lambda bi, i: (bi, i, 0))
    vec = _layer_weight((1, d))(idx)
    return pl.pallas_call(
        _conv_kernel,
        out_shape=jax.ShapeDtypeStruct((b, s, d), F32),
        grid=(b, s // CONV_TM),
        in_specs=[tile,
                  _layer_weight((1, d))(layer),
                  _layer_weight((d, 2 * d))(idx),
                  _layer_weight((1, 2 * d))(idx),
                  _layer_weight((CONV_WIDTH, d))(idx),
                  vec, vec, vec,
                  _layer_weight((d, d))(idx),
                  vec],
        out_specs=tile,
        scratch_shapes=[pltpu.VMEM((CONV_HALO + CONV_TM, d), F32),
                        pltpu.VMEM((SUBLANES - 1, CONV_HALO + CONV_TM - SUBLANES, d), F32),
                        pltpu.VMEM((CONV_TM, d), BF16)],
        compiler_params=_params(("parallel", "arbitrary")),
        name="conv_module",
    )(x, gain, w1, b1, wdw, bdw, lng, lnb, w2, b2)


def kernel(x, positions, norm_ffn1, norm_mix, norm_ffn2, ffn1_w_gate, ffn1_w_up, ffn1_w_down, ffn2_w_gate, ffn2_w_up, ffn2_w_down, fox_w_in, fox_b_f, fox_q_gain, fox_k_gain, fox_w_out, conv_w_pw1, conv_b_pw1, conv_w_dw, conv_b_dw, conv_ln_g, conv_ln_b, conv_w_pw2, conv_b_pw2, diff_w_in, diff_q_gain, diff_k_gain, diff_lambda_q1, diff_lambda_k1, diff_lambda_q2, diff_lambda_k2, diff_sub_gain, diff_w_out):
    b, s, d = x.shape
    depth = norm_ffn1.shape[0]
    assert d == D_MODEL and s % ATT_T == 0 and s % CONV_TM == 0 and (b * s) % FFN_TM == 0

    bf = lambda w: w.astype(BF16)
    row = lambda v: v[:, None, :]
    col = lambda v: v[:, :, None]
    g1, gm, g2 = row(norm_ffn1), row(norm_mix), row(norm_ffn2)
    f1 = (bf(ffn1_w_gate), bf(ffn1_w_up), bf(ffn1_w_down))
    f2 = (bf(ffn2_w_gate), bf(ffn2_w_up), bf(ffn2_w_down))
    fox_wqkv_t = bf(jnp.swapaxes(fox_w_in[:, :, :3 * d], 1, 2))
    fox_wf_t = bf(jnp.swapaxes(fox_w_in[:, :, 3 * d:], 1, 2))
    fox_wo = bf(fox_w_out)
    diff_wqkv_t = bf(jnp.swapaxes(diff_w_in, 1, 2))
    diff_wo = bf(diff_w_out)
    conv_w1, conv_w2 = bf(conv_w_pw1), bf(conv_w_pw2)
    tri = (jnp.arange(ATT_T)[:, None] <= jnp.arange(ATT_T)[None, :]).astype(BF16)
    inv_freq = (ROPE_THETA ** (-jnp.arange(0, ROT_DIM, 2, dtype=F32) / ROT_DIM))[:, None]
    pos = positions[:, None, :]

    flat = lambda a: a.reshape(b * s, d)
    i_fox = i_conv = i_diff = 0
    x2 = flat(x)
    for i in range(depth):
        x2 = _ffn(x2, g1, *f1, i)
        x3 = x2.reshape(b, s, d)
        kind = i % N_MIXERS
        if kind == 0:
            q_t, k, v_t = _fox_proj(x3, gm, fox_wqkv_t, fox_wf_t, col(fox_b_f),
                                    col(fox_q_gain), col(fox_k_gain), tri, i, i_fox)
            o = _fox_attn(q_t, k, v_t)
            x2 = _ffn(x2, g2, *f2, i, pre=(o, fox_wo, i_fox))
            i_fox += 1
        elif kind == 1:
            x3 = _conv_module(x3, gm, conv_w1, row(conv_b_pw1), conv_w_dw, row(conv_b_dw),
                              row(conv_ln_g), row(conv_ln_b), conv_w2, row(conv_b_pw2), i, i_conv)
            x2 = _ffn(flat(x3), g2, *f2, i)
            i_conv += 1
        else:
            lambda_init = 0.8 - 0.6 * math.exp(-0.3 * i)
            q_t, k, v_t = _diff_proj(x3, pos, gm, diff_wqkv_t, col(diff_q_gain),
                                     col(diff_k_gain), inv_freq, i, i_diff)
            o = _diff_attn(q_t, k, v_t, row(diff_lambda_q1), row(diff_lambda_k1),
                           row(diff_lambda_q2), row(diff_lambda_k2), col(diff_sub_gain),
                           i_diff, lambda_init)
            x2 = _ffn(x2, g2, *f2, i, pre=(o, diff_wo, i_diff))
            i_diff += 1
    return x2.reshape(b, s, d)
```

```python
import functools
import math

import jax
import jax.numpy as jnp
from jax import lax
from jax.experimental import pallas as pl
from jax.experimental.pallas import tpu as pltpu

D_MODEL = 1024
HEAD_DIM = 64
FOX_HEADS = D_MODEL // HEAD_DIM
DIFF_HEADS = D_MODEL // (2 * HEAD_DIM)
ROT_DIM = HEAD_DIM // 4
ROPE_THETA = 500000.0
CONV_WIDTH = 31
CHUNK = 64
N_MIXERS = 3
EPS = 1e-6
LOG2E = math.log2(math.e)
QSCALE = HEAD_DIM ** -0.5 * LOG2E

VMEM_LIMIT_BYTES = 56 * 1024 * 1024
SUBLANES = 8
SLAB = 128
AUG_ROWS = 16

FFN_TM = 1024
ATT_T = 512
ATT_TK = 256
ATT_TRIP = 16
CONV_TM = 256
CONV_RC = 32
CONV_HALO = 32
FFN_FC = 256

F32 = jnp.float32
BF16 = jnp.bfloat16
NT_DIMS = (((1,), (1,)), ((), ()))
TN_DIMS = (((0,), (0,)), ((), ()))


def _rms(x, g):
    return x * lax.rsqrt(jnp.mean(x * x, axis=-1, keepdims=True) + EPS) * g


def _resident(shape):
    nd = len(shape)
    return pl.BlockSpec(shape, lambda *_: (0,) * nd, pipeline_mode=pl.Buffered(1))


def _layer_weight(shape):
    nd = len(shape)

    def make(l):
        return pl.BlockSpec((None,) + tuple(shape), lambda *_: (l,) + (0,) * nd,
                            pipeline_mode=pl.Buffered(1))
    return make


def _params(sem):
    return pltpu.CompilerParams(dimension_semantics=sem, vmem_limit_bytes=VMEM_LIMIT_BYTES)


def _ffn_kernel(*refs, has_pre):
    if has_pre:
        o_ref, wo_ref, x_ref, g_ref, wg_ref, wu_ref, wd_ref, out_ref = refs
        proj = [lax.dot_general(o_ref[c], wo_ref[...], TN_DIMS, preferred_element_type=F32)
                for c in range(o_ref.shape[0])]
        x = x_ref[...] + jnp.concatenate(proj, axis=0)
    else:
        x_ref, g_ref, wg_ref, wu_ref, wd_ref, out_ref = refs
        x = x_ref[...]
    h = _rms(x, g_ref[...]).astype(BF16)
    d_ff = wg_ref.shape[1]
    y = jnp.zeros_like(x)
    for c in range(d_ff // FFN_FC):
        sl = slice(c * FFN_FC, (c + 1) * FFN_FC)
        gate = jnp.dot(h, wg_ref[:, sl], preferred_element_type=F32)
        up = jnp.dot(h, wu_ref[:, sl], preferred_element_type=F32)
        a = (gate * jax.nn.sigmoid(gate) * up).astype(BF16)
        y = y + jnp.dot(a, wd_ref[sl, :], preferred_element_type=F32)
    out_ref[...] = x + 0.5 * y


def _ffn(x2d, gain, wg, wu, wd, layer, pre=None):
    n, d = x2d.shape
    d_ff = wg.shape[-1]
    row = lambda i: (i, 0)
    in_specs = [pl.BlockSpec((FFN_TM, d), row),
                _layer_weight((1, d))(layer),
                _layer_weight((d, d_ff))(layer),
                _layer_weight((d, d_ff))(layer),
                _layer_weight((d_ff, d))(layer)]
    args = [x2d, gain, wg, wu, wd]
    if pre is not None:
        o_t, wo, wo_layer = pre
        per_row = o_t.shape[1] * ATT_T // FFN_TM
        o_spec = pl.BlockSpec((None, FFN_TM // ATT_T, d, ATT_T),
                              lambda i: (i // per_row, i % per_row, 0, 0))
        in_specs = [o_spec, _layer_weight((d, d))(wo_layer)] + in_specs
        args = [o_t, wo] + args
    return pl.pallas_call(
        functools.partial(_ffn_kernel, has_pre=pre is not None),
        out_shape=jax.ShapeDtypeStruct((n, d), F32),
        grid=(n // FFN_TM,),
        in_specs=in_specs,
        out_specs=pl.BlockSpec((FFN_TM, d), row),
        compiler_params=_params(("parallel",)),
        name="ffn_pre" if pre is not None else "ffn",
    )(*args)


def _head_norm_t(xt, gain_col):
    return xt * lax.rsqrt(jnp.mean(xt * xt, axis=0, keepdims=True) + EPS) * gain_col


def _split3(c):
    hi = c.astype(BF16).astype(F32)
    mid = (c - hi).astype(BF16).astype(F32)
    lo = (c - hi - mid).astype(BF16).astype(F32)
    return hi, mid, lo


def _fox_proj_kernel(x_ref, g_ref, wqkv_ref, wf_ref, bf_ref, qg_ref, kg_ref, tri_ref,
                     q_out, k_out, v_out, carry_ref):
    tm = x_ref.shape[0]

    @pl.when(pl.program_id(1) == 0)
    def _():
        carry_ref[...] = jnp.zeros_like(carry_ref)

    h = _rms(x_ref[...], g_ref[...]).astype(BF16)

    def project(part):
        return lax.dot_general(wqkv_ref[part * D_MODEL:(part + 1) * D_MODEL, :], h, NT_DIMS,
                               preferred_element_type=F32)

    f = lax.dot_general(wf_ref[...], h, NT_DIMS, preferred_element_type=F32) + bf_ref[...]
    kt_all = project(1)
    logf = jax.nn.log_sigmoid(f)
    pieces = jnp.concatenate(_split3(logf), axis=0).astype(BF16)
    part_sums = jnp.dot(pieces, tri_ref[...], preferred_element_type=F32)
    cum = carry_ref[...]
    for i in range(3):
        cum = cum + part_sums[i * FOX_HEADS:(i + 1) * FOX_HEADS, :]
    carry_ref[...] = cum[:, tm - 1:tm]

    row = lax.broadcasted_iota(jnp.int32, (AUG_ROWS, tm), 0)
    tail = SLAB - HEAD_DIM - AUG_ROWS
    bias_pieces = [_split3(cum[hh:hh + 1, :] * LOG2E) for hh in range(FOX_HEADS)]

    for hh in range(FOX_HEADS):
        c_hi, c_mid, c_lo = bias_pieces[hh]
        k_aug = jnp.where(row == 0, -c_hi,
                          jnp.where(row == 1, -c_mid, jnp.where(row == 2, -c_lo,
                                                                jnp.where(row < 6, 1.0, 0.0))))
        kt = kt_all[hh * HEAD_DIM:(hh + 1) * HEAD_DIM, :]
        k_slab = jnp.concatenate([_head_norm_t(kt, kg_ref[...]), k_aug, jnp.zeros((tail, tm), F32)],
                                 axis=0)
        k_out[:, hh * SLAB:(hh + 1) * SLAB] = k_slab.T.astype(BF16)

    qt_all = project(0)
    for hh in range(FOX_HEADS):
        c_hi, c_mid, c_lo = bias_pieces[hh]
        q_aug = jnp.where(row < 3, 1.0,
                          jnp.where(row == 3, c_hi, jnp.where(row == 4, c_mid,
                                                              jnp.where(row == 5, c_lo, 0.0))))
        qt = qt_all[hh * HEAD_DIM:(hh + 1) * HEAD_DIM, :]
        q_out[hh, 0:HEAD_DIM, :] = (_head_norm_t(qt, qg_ref[...]) * QSCALE).astype(BF16)
        q_out[hh, HEAD_DIM:HEAD_DIM + AUG_ROWS, :] = q_aug.astype(BF16)
        q_out[hh, HEAD_DIM + AUG_ROWS:SLAB, :] = jnp.zeros((tail, tm), BF16)

    vt_all = project(2)
    for c in range(ATT_T // ATT_TK):
        v_out[c] = vt_all[:, c * ATT_TK:(c + 1) * ATT_TK].astype(BF16)


def _fox_proj(x, gain, wqkv_t, wf_t, b_f, q_gain, k_gain, tri, layer, idx):
    b, s, d = x.shape
    nt = s // ATT_T
    tile = lambda bi, i: (bi, i, 0)
    return pl.pallas_call(
        _fox_proj_kernel,
        out_shape=(jax.ShapeDtypeStruct((b, nt, FOX_HEADS, SLAB, ATT_T), BF16),
                   jax.ShapeDtypeStruct((b, s, FOX_HEADS * SLAB), BF16),
                   jax.ShapeDtypeStruct((b, s // ATT_TK, d, ATT_TK), BF16)),
        grid=(b, nt),
        in_specs=[pl.BlockSpec((None, ATT_T, d), tile),
                  _layer_weight((1, d))(layer),
                  _layer_weight((3 * d, d))(idx),
                  _layer_weight((FOX_HEADS, d))(idx),
                  _layer_weight((FOX_HEADS, 1))(idx),
                  _layer_weight((HEAD_DIM, 1))(idx),
                  _layer_weight((HEAD_DIM, 1))(idx),
                  _resident((ATT_T, ATT_T))],
        out_specs=(pl.BlockSpec((None, None, FOX_HEADS, SLAB, ATT_T), lambda bi, i: (bi, i, 0, 0, 0)),
                   pl.BlockSpec((None, ATT_T, FOX_HEADS * SLAB), tile),
                   pl.BlockSpec((None, ATT_T // ATT_TK, d, ATT_TK), lambda bi, i: (bi, i, 0, 0))),
        scratch_shapes=[pltpu.VMEM((FOX_HEADS, 1), F32)],
        compiler_params=_params(("parallel", "arbitrary")),
        name="fox_proj",
    )(x, gain, wqkv_t, wf_t, b_f, q_gain, k_gain, tri)


def _attend(q_of, k_of, v_of, bias_ref, s_ref, smax_ref, m_ref, l_ref, acc_ref, qi):
    sub = ATT_T // ATT_TK
    assert sub % 2 == 0 and ATT_TRIP % 2 == 0 and ATT_TRIP % sub == 0
    n_full = qi * sub
    m_ref[...] = jnp.full_like(m_ref, -1e30)
    l_ref[...] = jnp.zeros_like(l_ref)
    acc_ref[...] = jnp.zeros_like(acc_ref)

    def scores(slot, ki, c0=0):
        for j in range(2):
            s = jnp.dot(k_of(ki, j), q_of(j, c0), preferred_element_type=F32)
            s_ref[slot, j, :, c0:] = s
            smax_ref[slot, j, :, c0:] = jnp.max(s, axis=0, keepdims=True)

    def update(slot, ki, diag=None, c0=0):
        for j in range(2):
            s = s_ref[slot, j, :, c0:]
            if diag is None:
                s_max = smax_ref[slot, j, :, c0:]
            else:
                s = s + bias_ref[diag, :, c0:]
                s_max = jnp.max(s, axis=0, keepdims=True)
            m_prev = m_ref[j, :, c0:]
            m_new = jnp.maximum(m_prev, s_max)
            alpha = jnp.exp2(m_prev - m_new)
            p = jnp.exp2(s - m_new)
            l_ref[j, :, c0:] = alpha * l_ref[j, :, c0:] + jnp.sum(p, axis=0, keepdims=True)
            acc_ref[j, :, c0:] = alpha * acc_ref[j, :, c0:] + jnp.dot(
                v_of(ki, j), p.astype(BF16), preferred_element_type=F32)
            m_ref[j, :, c0:] = m_new

    def run(first, count):
        for c in range(count):
            scores((c + 1) % 2, first + c + 1)
            update(c % 2, first + c)

    def body(i, carry):
        run(i * ATT_TRIP, ATT_TRIP)
        return carry

    def tail(first, full):
        for c in range(full + sub):
            if c + 1 < full + sub:
                scores((c + 1) % 2, first + c + 1, max(c + 1 - full, 0) * ATT_TK)
            if c < full:
                update(c % 2, first + c)
            else:
                update(c % 2, first + c, c - full, (c - full) * ATT_TK)

    scores(0, 0)
    trips = n_full // ATT_TRIP
    lax.fori_loop(0, trips, body, 0)
    for rem in range(0, ATT_TRIP, sub):
        @pl.when(n_full - trips * ATT_TRIP == rem)
        def _():
            tail(trips * ATT_TRIP, rem)


def _attn_scratch(dv):
    return [pltpu.VMEM((ATT_T // ATT_TK, ATT_TK, ATT_T), F32),
            pltpu.VMEM((2, 2, ATT_TK, ATT_T), F32),
            pltpu.VMEM((2, 2, 1, ATT_T), F32),
            pltpu.VMEM((2, 1, ATT_T), F32),
            pltpu.VMEM((2, 1, ATT_T), F32),
            pltpu.VMEM((2, dv, ATT_T), F32)]


def _key_rows(ki):
    return pl.ds(pl.multiple_of(ki * ATT_TK, ATT_TK), ATT_TK)


def _fill_diag_bias(bias_ref, visible):
    for d in range(bias_ref.shape[0]):
        key = d * ATT_TK + lax.broadcasted_iota(jnp.int32, (ATT_TK, ATT_T), 0)
        qry = lax.broadcasted_iota(jnp.int32, (ATT_TK, ATT_T), 1)
        bias_ref[d] = jnp.where(visible(key, qry), 0.0, -jnp.inf)


def _fox_attn_kernel(q_ref, k_ref, v_ref, o_ref, bias_ref, s_ref, smax_ref, m_ref, l_ref, acc_ref):
    _fill_diag_bias(bias_ref, lambda key, qry: key <= qry)

    def query_tile(qi, carry):
        _attend(lambda j, c0: q_ref[qi, j, :, c0:],
                lambda ki, j: k_ref[_key_rows(ki), j * SLAB:(j + 1) * SLAB],
                lambda ki, j: v_ref[ki, j * HEAD_DIM:(j + 1) * HEAD_DIM, :],
                bias_ref, s_ref, smax_ref, m_ref, l_ref, acc_ref, qi)
        o_t = jnp.concatenate([acc_ref[0] / l_ref[0], acc_ref[1] / l_ref[1]], axis=0)
        o_ref[qi] = o_t.astype(BF16)
        return carry

    lax.fori_loop(0, q_ref.shape[0], query_tile, 0)


def _fox_attn(q_t, k, v_t):
    b, nk, d, _ = v_t.shape
    s = nk * ATT_TK
    nt = s // ATT_T
    pair = 2 * HEAD_DIM
    return pl.pallas_call(
        _fox_attn_kernel,
        out_shape=jax.ShapeDtypeStruct((b, nt, d, ATT_T), BF16),
        grid=(b, FOX_HEADS // 2),
        in_specs=[pl.BlockSpec((None, nt, 2, SLAB, ATT_T), lambda bi, hp: (bi, 0, hp, 0, 0)),
                  pl.BlockSpec((None, s, 2 * SLAB), lambda bi, hp: (bi, 0, hp)),
                  pl.BlockSpec((None, nk, pair, ATT_TK), lambda bi, hp: (bi, 0, hp, 0))],
        out_specs=pl.BlockSpec((None, nt, pair, ATT_T), lambda bi, hp: (bi, 0, hp, 0)),
        scratch_shapes=_attn_scratch(HEAD_DIM),
        compiler_params=_params(("parallel", "parallel")),
        name="fox_attn",
    )(q_t, k, v_t)


def _diff_proj_kernel(x_ref, pos_ref, g_ref, wqkv_ref, qg_ref, kg_ref, invf_ref,
                      q_out, k_out, v_out):
    half = ROT_DIM // 2
    h = _rms(x_ref[...], g_ref[...]).astype(BF16)
    ang = invf_ref[...] * pos_ref[...].astype(F32)
    cos = jnp.cos(ang)
    sin = jnp.sin(ang)

    def project(part):
        return lax.dot_general(wqkv_ref[part * D_MODEL:(part + 1) * D_MODEL, :], h, NT_DIMS,
                               preferred_element_type=F32)

    def rope_t(xt):
        x1, x2 = xt[0:half], xt[half:ROT_DIM]
        return jnp.concatenate([x1 * cos - x2 * sin, x2 * cos + x1 * sin, xt[ROT_DIM:]], axis=0)

    def head_t(xt_all, hh, gain_ref):
        maps = [rope_t(_head_norm_t(xt_all[hh * SLAB + mp * HEAD_DIM:hh * SLAB + (mp + 1) * HEAD_DIM, :],
                                    gain_ref[...])) for mp in range(2)]
        return jnp.concatenate(maps, axis=0)

    kt_all = project(1)
    for hh in range(DIFF_HEADS):
        k_out[:, hh * SLAB:(hh + 1) * SLAB] = head_t(kt_all, hh, kg_ref).T.astype(BF16)
    qt_all = project(0)
    for hh in range(DIFF_HEADS):
        q_out[hh] = (head_t(qt_all, hh, qg_ref) * QSCALE).astype(BF16)
    vt_all = project(2)
    for c in range(ATT_T // ATT_TK):
        v_out[c] = vt_all[:, c * ATT_TK:(c + 1) * ATT_TK].astype(BF16)


def _diff_proj(x, pos, gain, wqkv_t, q_gain, k_gain, inv_freq, layer, idx):
    b, s, d = x.shape
    nt = s // ATT_T
    tile = lambda bi, i: (bi, i, 0)
    return pl.pallas_call(
        _diff_proj_kernel,
        out_shape=(jax.ShapeDtypeStruct((b, nt, DIFF_HEADS, SLAB, ATT_T), BF16),
                   jax.ShapeDtypeStruct((b, s, d), BF16),
                   jax.ShapeDtypeStruct((b, s // ATT_TK, d, ATT_TK), BF16)),
        grid=(b, nt),
        in_specs=[pl.BlockSpec((None, ATT_T, d), tile),
                  pl.BlockSpec((None, 1, ATT_T), lambda bi, i: (bi, 0, i)),
                  _layer_weight((1, d))(layer),
                  _layer_weight((3 * d, d))(idx),
                  _layer_weight((HEAD_DIM, 1))(idx),
                  _layer_weight((HEAD_DIM, 1))(idx),
                  _resident((ROT_DIM // 2, 1))],
        out_specs=(pl.BlockSpec((None, None, DIFF_HEADS, SLAB, ATT_T), lambda bi, i: (bi, i, 0, 0, 0)),
                   pl.BlockSpec((None, ATT_T, d), tile),
                   pl.BlockSpec((None, ATT_T // ATT_TK, d, ATT_TK), lambda bi, i: (bi, i, 0, 0))),
        compiler_params=_params(("parallel", "parallel")),
        name="diff_proj",
    )(x, pos, gain, wqkv_t, q_gain, k_gain, inv_freq)


def _diff_attn_kernel(q_ref, k_ref, v_ref, lq1_ref, lk1_ref, lq2_ref, lk2_ref, sg_ref, o_ref,
                      qm_ref, bias_ref, s_ref, smax_ref, m_ref, l_ref, acc_ref, *, lambda_init):
    lam = (jnp.exp(jnp.sum(lq1_ref[...] * lk1_ref[...], axis=1, keepdims=True))
           - jnp.exp(jnp.sum(lq2_ref[...] * lk2_ref[...], axis=1, keepdims=True)) + lambda_init)
    feat = lax.broadcasted_iota(jnp.int32, (SLAB, ATT_T), 0)
    _fill_diag_bias(bias_ref, lambda key, qry: (key // CHUNK) <= (qry // CHUNK))

    def query_tile(qi, carry):
        q = q_ref[qi].astype(F32)
        qm_ref[0] = jnp.where(feat < HEAD_DIM, q, 0.0).astype(BF16)
        qm_ref[1] = jnp.where(feat >= HEAD_DIM, q, 0.0).astype(BF16)
        _attend(lambda j, c0: qm_ref[j, :, c0:],
                lambda ki, j: k_ref[_key_rows(ki), :],
                lambda ki, j: v_ref[ki],
                bias_ref, s_ref, smax_ref, m_ref, l_ref, acc_ref, qi)
        o_t = acc_ref[0] / l_ref[0] - lam * (acc_ref[1] / l_ref[1])
        o_t = o_t * lax.rsqrt(jnp.mean(o_t * o_t, axis=0, keepdims=True) + EPS) * sg_ref[...]
        o_ref[qi] = (o_t * (1.0 - lambda_init)).astype(BF16)
        return carry

    lax.fori_loop(0, q_ref.shape[0], query_tile, 0)


def _diff_attn(q_t, k, v_t, lq1, lk1, lq2, lk2, sub_gain, idx, lambda_init):
    b, nk, d, _ = v_t.shape
    s = nk * ATT_TK
    nt = s // ATT_T
    vec = _layer_weight((1, HEAD_DIM))(idx)
    return pl.pallas_call(
        functools.partial(_diff_attn_kernel, lambda_init=lambda_init),
        out_shape=jax.ShapeDtypeStruct((b, nt, d, ATT_T), BF16),
        grid=(b, DIFF_HEADS),
        in_specs=[pl.BlockSpec((None, nt, None, SLAB, ATT_T), lambda bi, hh: (bi, 0, hh, 0, 0)),
                  pl.BlockSpec((None, s, SLAB), lambda bi, hh: (bi, 0, hh)),
                  pl.BlockSpec((None, nk, SLAB, ATT_TK), lambda bi, hh: (bi, 0, hh, 0)),
                  vec, vec, vec, vec,
                  _layer_weight((2 * HEAD_DIM, 1))(idx)],
        out_specs=pl.BlockSpec((None, nt, SLAB, ATT_T), lambda bi, hh: (bi, 0, hh, 0)),
        scratch_shapes=[pltpu.VMEM((2, SLAB, ATT_T), BF16)] + _attn_scratch(2 * HEAD_DIM),
        compiler_params=_params(("parallel", "parallel")),
        name="diff_attn",
    )(q_t, k, v_t, lq1, lk1, lq2, lk2, sub_gain)


def _conv_kernel(x_ref, g_ref, w1_ref, b1_ref, wdw_ref, bdw_ref, lng_ref, lnb_ref, w2_ref, b2_ref,
                 out_ref, u_ref, us_ref, a_ref):
    tm, d = x_ref.shape

    @pl.when(pl.program_id(1) == 0)
    def _():
        u_ref[0:CONV_HALO, :] = jnp.zeros((CONV_HALO, d), F32)

    @pl.when(pl.program_id(1) > 0)
    def _():
        u_ref[0:CONV_HALO, :] = u_ref[tm:tm + CONV_HALO, :]

    x = x_ref[...]
    h = _rms(x, g_ref[...]).astype(BF16)
    u = jnp.dot(h, w1_ref[...], preferred_element_type=F32) + b1_ref[...]
    u_ref[CONV_HALO:CONV_HALO + tm, :] = u[:, :d] * jax.nn.sigmoid(u[:, d:])
    span = us_ref.shape[1]
    for b in range(1, SUBLANES):
        us_ref[b - 1] = u_ref[b:b + span, :]

    first = CONV_HALO - (CONV_WIDTH - 1)
    for c in range(tm // CONV_RC):
        base = c * CONV_RC
        acc = jnp.broadcast_to(bdw_ref[...], (CONV_RC, d))
        for j in range(CONV_WIDTH):
            b = (first + j) % SUBLANES
            lo = base + first + j - b
            rows = u_ref[lo:lo + CONV_RC, :] if b == 0 else us_ref[b - 1, lo:lo + CONV_RC, :]
            acc = acc + wdw_ref[j:j + 1, :] * rows
        mu = jnp.mean(acc, axis=-1, keepdims=True)
        cen = acc - mu
        var = jnp.mean(cen * cen, axis=-1, keepdims=True)
        y = cen * lax.rsqrt(var + EPS) * lng_ref[...] + lnb_ref[...]
        a_ref[base:base + CONV_RC, :] = (y * jax.nn.sigmoid(y)).astype(BF16)

    out_ref[...] = x + jnp.dot(a_ref[...], w2_ref[...], preferred_element_type=F32) + b2_ref[...]


def _conv_module(x, gain, w1, b1, wdw, bdw, lng, lnb, w2, b2, layer, idx):
    b, s, d = x.shape
    tile = pl.BlockSpec((None, CONV_TM, d), lambda bi, i: (bi, i, 0))
    vec = _layer_weight((1, d))(idx)
    return pl.pallas_call(
        _conv_kernel,
        out_shape=jax.ShapeDtypeStruct((b, s, d), F32),
        grid=(b, s // CONV_TM),
        in_specs=[tile,
                  _layer_weight((1, d))(layer),
                  _layer_weight((d, 2 * d))(idx),
                  _layer_weight((1, 2 * d))(idx),
                  _layer_weight((CONV_WIDTH, d))(idx),
                  vec, vec, vec,
                  _layer_weight((d, d))(idx),
                  vec],
        out_specs=tile,
        scratch_shapes=[pltpu.VMEM((CONV_HALO + CONV_TM, d), F32),
                        pltpu.VMEM((SUBLANES - 1, CONV_HALO + CONV_TM - SUBLANES, d), F32),
                        pltpu.VMEM((CONV_TM, d), BF16)],
        compiler_params=_params(("parallel", "arbitrary")),
        name="conv_module",
    )(x, gain, w1, b1, wdw, bdw, lng, lnb, w2, b2)


def kernel(x, positions, norm_ffn1, norm_mix, norm_ffn2, ffn1_w_gate, ffn1_w_up, ffn1_w_down, ffn2_w_gate, ffn2_w_up, ffn2_w_down, fox_w_in, fox_b_f, fox_q_gain, fox_k_gain, fox_w_out, conv_w_pw1, conv_b_pw1, conv_w_dw, conv_b_dw, conv_ln_g, conv_ln_b, conv_w_pw2, conv_b_pw2, diff_w_in, diff_q_gain, diff_k_gain, diff_lambda_q1, diff_lambda_k1, diff_lambda_q2, diff_lambda_k2, diff_sub_gain, diff_w_out):
    b, s, d = x.shape
    depth = norm_ffn1.shape[0]
    assert d == D_MODEL and s % ATT_T == 0 and s % CONV_TM == 0 and (b * s) % FFN_TM == 0

    bf = lambda w: w.astype(BF16)
    row = lambda v: v[:, None, :]
    col = lambda v: v[:, :, None]
    g1, gm, g2 = row(norm_ffn1), row(norm_mix), row(norm_ffn2)
    f1 = (bf(ffn1_w_gate), bf(ffn1_w_up), bf(ffn1_w_down))
    f2 = (bf(ffn2_w_gate), bf(ffn2_w_up), bf(ffn2_w_down))
    fox_wqkv_t = bf(jnp.swapaxes(fox_w_in[:, :, :3 * d], 1, 2))
    fox_wf_t = bf(jnp.swapaxes(fox_w_in[:, :, 3 * d:], 1, 2))
    fox_wo = bf(fox_w_out)
    diff_wqkv_t = bf(jnp.swapaxes(diff_w_in, 1, 2))
    diff_wo = bf(diff_w_out)
    conv_w1, conv_w2 = bf(conv_w_pw1), bf(conv_w_pw2)
    tri = (jnp.arange(ATT_T)[:, None] <= jnp.arange(ATT_T)[None, :]).astype(BF16)
    inv_freq = (ROPE_THETA ** (-jnp.arange(0, ROT_DIM, 2, dtype=F32) / ROT_DIM))[:, None]
    pos = positions[:, None, :]

    flat = lambda a: a.reshape(b * s, d)
    i_fox = i_conv = i_diff = 0
    x2 = flat(x)
    for i in range(depth):
        x2 = _ffn(x2, g1, *f1, i)
        x3 = x2.reshape(b, s, d)
        kind = i % N_MIXERS
        if kind == 0:
            q_t, k, v_t = _fox_proj(x3, gm, fox_wqkv_t, fox_wf_t, col(fox_b_f),
                                    col(fox_q_gain), col(fox_k_gain), tri, i, i_fox)
            o = _fox_attn(q_t, k, v_t)
            x2 = _ffn(x2, g2, *f2, i, pre=(o, fox_wo, i_fox))
            i_fox += 1
        elif kind == 1:
            x3 = _conv_module(x3, gm, conv_w1, row(conv_b_pw1), conv_w_dw, row(conv_b_dw),
                              row(conv_ln_g), row(conv_ln_b), conv_w2, row(conv_b_pw2), i, i_conv)
            x2 = _ffn(flat(x3), g2, *f2, i)
            i_conv += 1
        else:
            lambda_init = 0.8 - 0.6 * math.exp(-0.3 * i)
            q_t, k, v_t = _diff_proj(x3, pos, gm, diff_wqkv_t, col(diff_q_gain),
                                     col(diff_k_gain), inv_freq, i, i_diff)
            o = _diff_attn(q_t, k, v_t, row(diff_lambda_q1), row(diff_lambda_k1),
                           row(diff_lambda_q2), row(diff_lambda_k2), col(diff_sub_gain),
                           i_diff, lambda_init)
            x2 = _ffn(x2, g2, *f2, i, pre=(o, diff_wo, i_diff))
            i_diff += 1
    return x2.reshape(b, s, d)
```

```python
import functools
import math

import jax
import jax.numpy as jnp
from jax import lax
from jax.experimental import pallas as pl
from jax.experimental.pallas import tpu as pltpu

D_MODEL = 1024
HEAD_DIM = 64
FOX_HEADS = D_MODEL // HEAD_DIM
DIFF_HEADS = D_MODEL // (2 * HEAD_DIM)
ROT_DIM = HEAD_DIM // 4
ROPE_THETA = 500000.0
CONV_WIDTH = 31
CHUNK = 64
N_MIXERS = 3
EPS = 1e-6
LOG2E = math.log2(math.e)
QSCALE = HEAD_DIM ** -0.5 * LOG2E

VMEM_LIMIT_BYTES = 56 * 1024 * 1024
SUBLANES = 8
SLAB = 128
AUG_ROWS = 16

FFN_TM = 1024
ATT_T = 512
ATT_TK = 256
ATT_TRIP = 16
CONV_TM = 512
CONV_RC = 32
CONV_HALO = 32
FFN_FC = 256

F32 = jnp.float32
BF16 = jnp.bfloat16
NT_DIMS = (((1,), (1,)), ((), ()))
TN_DIMS = (((0,), (0,)), ((), ()))


def _rms(x, g):
    return x * lax.rsqrt(jnp.mean(x * x, axis=-1, keepdims=True) + EPS) * g


def _resident(shape):
    nd = len(shape)
    return pl.BlockSpec(shape, lambda *_: (0,) * nd, pipeline_mode=pl.Buffered(1))


def _layer_weight(shape):
    nd = len(shape)

    def make(l):
        return pl.BlockSpec((None,) + tuple(shape), lambda *_: (l,) + (0,) * nd,
                            pipeline_mode=pl.Buffered(1))
    return make


def _params(sem):
    return pltpu.CompilerParams(dimension_semantics=sem, vmem_limit_bytes=VMEM_LIMIT_BYTES)


def _ffn_kernel(*refs, has_pre):
    if has_pre:
        o_ref, wo_ref, x_ref, g_ref, wg_ref, wu_ref, wd_ref, out_ref = refs
        proj = [lax.dot_general(o_ref[c], wo_ref[...], TN_DIMS, preferred_element_type=F32)
                for c in range(o_ref.shape[0])]
        x = x_ref[...] + jnp.concatenate(proj, axis=0)
    else:
        x_ref, g_ref, wg_ref, wu_ref, wd_ref, out_ref = refs
        x = x_ref[...]
    h = _rms(x, g_ref[...]).astype(BF16)
    d_ff = wg_ref.shape[1]
    y = jnp.zeros_like(x)
    for c in range(d_ff // FFN_FC):
        sl = slice(c * FFN_FC, (c + 1) * FFN_FC)
        gate = jnp.dot(h, wg_ref[:, sl], preferred_element_type=F32)
        up = jnp.dot(h, wu_ref[:, sl], preferred_element_type=F32)
        a = (gate * jax.nn.sigmoid(gate) * up).astype(BF16)
        y = y + jnp.dot(a, wd_ref[sl, :], preferred_element_type=F32)
    out_ref[...] = x + 0.5 * y


def _ffn(x2d, gain, wg, wu, wd, layer, pre=None):
    n, d = x2d.shape
    d_ff = wg.shape[-1]
    row = lambda i: (i, 0)
    in_specs = [pl.BlockSpec((FFN_TM, d), row),
                _layer_weight((1, d))(layer),
                _layer_weight((d, d_ff))(layer),
                _layer_weight((d, d_ff))(layer),
                _layer_weight((d_ff, d))(layer)]
    args = [x2d, gain, wg, wu, wd]
    if pre is not None:
        o_t, wo, wo_layer = pre
        per_row = o_t.shape[1] * ATT_T // FFN_TM
        o_spec = pl.BlockSpec((None, FFN_TM // ATT_T, d, ATT_T),
                              lambda i: (i // per_row, i % per_row, 0, 0))
        in_specs = [o_spec, _layer_weight((d, d))(wo_layer)] + in_specs
        args = [o_t, wo] + args
    return pl.pallas_call(
        functools.partial(_ffn_kernel, has_pre=pre is not None),
        out_shape=jax.ShapeDtypeStruct((n, d), F32),
        grid=(n // FFN_TM,),
        in_specs=in_specs,
        out_specs=pl.BlockSpec((FFN_TM, d), row),
        compiler_params=_params(("parallel",)),
        name="ffn_pre" if pre is not None else "ffn",
    )(*args)


def _head_norm_t(xt, gain_col):
    return xt * lax.rsqrt(jnp.mean(xt * xt, axis=0, keepdims=True) + EPS) * gain_col


def _split3(c):
    hi = c.astype(BF16).astype(F32)
    mid = (c - hi).astype(BF16).astype(F32)
    lo = (c - hi - mid).astype(BF16).astype(F32)
    return hi, mid, lo


def _fox_proj_kernel(x_ref, g_ref, wqkv_ref, wf_ref, bf_ref, qg_ref, kg_ref, tri_ref,
                     q_out, k_out, v_out, carry_ref):
    tm = x_ref.shape[0]

    @pl.when(pl.program_id(1) == 0)
    def _():
        carry_ref[...] = jnp.zeros_like(carry_ref)

    h = _rms(x_ref[...], g_ref[...]).astype(BF16)

    def project(part):
        return lax.dot_general(wqkv_ref[part * D_MODEL:(part + 1) * D_MODEL, :], h, NT_DIMS,
                               preferred_element_type=F32)

    f = lax.dot_general(wf_ref[...], h, NT_DIMS, preferred_element_type=F32) + bf_ref[...]
    kt_all = project(1)
    logf = jax.nn.log_sigmoid(f)
    pieces = jnp.concatenate(_split3(logf), axis=0).astype(BF16)
    part_sums = jnp.dot(pieces, tri_ref[...], preferred_element_type=F32)
    cum = carry_ref[...]
    for i in range(3):
        cum = cum + part_sums[i * FOX_HEADS:(i + 1) * FOX_HEADS, :]
    carry_ref[...] = cum[:, tm - 1:tm]

    row = lax.broadcasted_iota(jnp.int32, (AUG_ROWS, tm), 0)
    tail = SLAB - HEAD_DIM - AUG_ROWS
    bias_pieces = [_split3(cum[hh:hh + 1, :] * LOG2E) for hh in range(FOX_HEADS)]

    for hh in range(FOX_HEADS):
        c_hi, c_mid, c_lo = bias_pieces[hh]
        k_aug = jnp.where(row == 0, -c_hi,
                          jnp.where(row == 1, -c_mid, jnp.where(row == 2, -c_lo,
                                                                jnp.where(row < 6, 1.0, 0.0))))
        kt = kt_all[hh * HEAD_DIM:(hh + 1) * HEAD_DIM, :]
        k_slab = jnp.concatenate([_head_norm_t(kt, kg_ref[...]), k_aug, jnp.zeros((tail, tm), F32)],
                                 axis=0)
        k_out[:, hh * SLAB:(hh + 1) * SLAB] = k_slab.T.astype(BF16)

    qt_all = project(0)
    for hh in range(FOX_HEADS):
        c_hi, c_mid, c_lo = bias_pieces[hh]
        q_aug = jnp.where(row < 3, 1.0,
                          jnp.where(row == 3, c_hi, jnp.where(row == 4, c_mid,
                                                              jnp.where(row == 5, c_lo, 0.0))))
        qt = qt_all[hh * HEAD_DIM:(hh + 1) * HEAD_DIM, :]
        q_out[hh, 0:HEAD_DIM, :] = (_head_norm_t(qt, qg_ref[...]) * QSCALE).astype(BF16)
        q_out[hh, HEAD_DIM:HEAD_DIM + AUG_ROWS, :] = q_aug.astype(BF16)
        q_out[hh, HEAD_DIM + AUG_ROWS:SLAB, :] = jnp.zeros((tail, tm), BF16)

    vt_all = project(2)
    for c in range(ATT_T // ATT_TK):
        v_out[c] = vt_all[:, c * ATT_TK:(c + 1) * ATT_TK].astype(BF16)


def _fox_proj(x, gain, wqkv_t, wf_t, b_f, q_gain, k_gain, tri, layer, idx):
    b, s, d = x.shape
    nt = s // ATT_T
    tile = lambda bi, i: (bi, i, 0)
    return pl.pallas_call(
        _fox_proj_kernel,
        out_shape=(jax.ShapeDtypeStruct((b, nt, FOX_HEADS, SLAB, ATT_T), BF16),
                   jax.ShapeDtypeStruct((b, s, FOX_HEADS * SLAB), BF16),
                   jax.ShapeDtypeStruct((b, s // ATT_TK, d, ATT_TK), BF16)),
        grid=(b, nt),
        in_specs=[pl.BlockSpec((None, ATT_T, d), tile),
                  _layer_weight((1, d))(layer),
                  _layer_weight((3 * d, d))(idx),
                  _layer_weight((FOX_HEADS, d))(idx),
                  _layer_weight((FOX_HEADS, 1))(idx),
                  _layer_weight((HEAD_DIM, 1))(idx),
                  _layer_weight((HEAD_DIM, 1))(idx),
                  _resident((ATT_T, ATT_T))],
        out_specs=(pl.BlockSpec((None, None, FOX_HEADS, SLAB, ATT_T), lambda bi, i: (bi, i, 0, 0, 0)),
                   pl.BlockSpec((None, ATT_T, FOX_HEADS * SLAB), tile),
                   pl.BlockSpec((None, ATT_T // ATT_TK, d, ATT_TK), lambda bi, i: (bi, i, 0, 0))),
        scratch_shapes=[pltpu.VMEM((FOX_HEADS, 1), F32)],
        compiler_params=_params(("parallel", "arbitrary")),
        name="fox_proj",
    )(x, gain, wqkv_t, wf_t, b_f, q_gain, k_gain, tri)


def _attend(q_of, k_of, v_of, bias_ref, s_ref, smax_ref, m_ref, l_ref, acc_ref, qi):
    sub = ATT_T // ATT_TK
    assert sub % 2 == 0 and ATT_TRIP % 2 == 0 and ATT_TRIP % sub == 0
    n_full = qi * sub
    m_ref[...] = jnp.full_like(m_ref, -1e30)
    l_ref[...] = jnp.zeros_like(l_ref)
    acc_ref[...] = jnp.zeros_like(acc_ref)

    def scores(slot, ki, c0=0):
        for j in range(2):
            s = jnp.dot(k_of(ki, j), q_of(j, c0), preferred_element_type=F32)
            s_ref[slot, j, :, c0:] = s
            smax_ref[slot, j, :, c0:] = jnp.max(s, axis=0, keepdims=True)

    def update(slot, ki, diag=None, c0=0):
        for j in range(2):
            s = s_ref[slot, j, :, c0:]
            if diag is None:
                s_max = smax_ref[slot, j, :, c0:]
            else:
                s = s + bias_ref[diag, :, c0:]
                s_max = jnp.max(s, axis=0, keepdims=True)
            m_prev = m_ref[j, :, c0:]
            m_new = jnp.maximum(m_prev, s_max)
            alpha = jnp.exp2(m_prev - m_new)
            p = jnp.exp2(s - m_new)
            l_ref[j, :, c0:] = alpha * l_ref[j, :, c0:] + jnp.sum(p, axis=0, keepdims=True)
            acc_ref[j, :, c0:] = alpha * acc_ref[j, :, c0:] + jnp.dot(
                v_of(ki, j), p.astype(BF16), preferred_element_type=F32)
            m_ref[j, :, c0:] = m_new

    def run(first, count):
        for c in range(count):
            scores((c + 1) % 2, first + c + 1)
            update(c % 2, first + c)

    def body(i, carry):
        run(i * ATT_TRIP, ATT_TRIP)
        return carry

    def tail(first, full):
        for c in range(full + sub):
            if c + 1 < full + sub:
                scores((c + 1) % 2, first + c + 1, max(c + 1 - full, 0) * ATT_TK)
            if c < full:
                update(c % 2, first + c)
            else:
                update(c % 2, first + c, c - full, (c - full) * ATT_TK)

    scores(0, 0)
    trips = n_full // ATT_TRIP
    lax.fori_loop(0, trips, body, 0)
    for rem in range(0, ATT_TRIP, sub):
        @pl.when(n_full - trips * ATT_TRIP == rem)
        def _():
            tail(trips * ATT_TRIP, rem)


def _attn_scratch(dv):
    return [pltpu.VMEM((ATT_T // ATT_TK, ATT_TK, ATT_T), F32),
            pltpu.VMEM((2, 2, ATT_TK, ATT_T), F32),
            pltpu.VMEM((2, 2, 1, ATT_T), F32),
            pltpu.VMEM((2, 1, ATT_T), F32),
            pltpu.VMEM((2, 1, ATT_T), F32),
            pltpu.VMEM((2, dv, ATT_T), F32)]


def _key_rows(ki):
    return pl.ds(pl.multiple_of(ki * ATT_TK, ATT_TK), ATT_TK)


def _fill_diag_bias(bias_ref, visible):
    for d in range(bias_ref.shape[0]):
        key = d * ATT_TK + lax.broadcasted_iota(jnp.int32, (ATT_TK, ATT_T), 0)
        qry = lax.broadcasted_iota(jnp.int32, (ATT_TK, ATT_T), 1)
        bias_ref[d] = jnp.where(visible(key, qry), 0.0, -jnp.inf)


def _fox_attn_kernel(q_ref, k_ref, v_ref, o_ref, bias_ref, s_ref, smax_ref, m_ref, l_ref, acc_ref):
    _fill_diag_bias(bias_ref, lambda key, qry: key <= qry)

    def query_tile(qi, carry):
        _attend(lambda j, c0: q_ref[qi, j, :, c0:],
                lambda ki, j: k_ref[_key_rows(ki), j * SLAB:(j + 1) * SLAB],
                lambda ki, j: v_ref[ki, j * HEAD_DIM:(j + 1) * HEAD_DIM, :],
                bias_ref, s_ref, smax_ref, m_ref, l_ref, acc_ref, qi)
        o_t = jnp.concatenate([acc_ref[0] / l_ref[0], acc_ref[1] / l_ref[1]], axis=0)
        o_ref[qi] = o_t.astype(BF16)
        return carry

    lax.fori_loop(0, q_ref.shape[0], query_tile, 0)


def _fox_attn(q_t, k, v_t):
    b, nk, d, _ = v_t.shape
    s = nk * ATT_TK
    nt = s // ATT_T
    pair = 2 * HEAD_DIM
    return pl.pallas_call(
        _fox_attn_kernel,
        out_shape=jax.ShapeDtypeStruct((b, nt, d, ATT_T), BF16),
        grid=(b, FOX_HEADS // 2),
        in_specs=[pl.BlockSpec((None, nt, 2, SLAB, ATT_T), lambda bi, hp: (bi, 0, hp, 0, 0)),
                  pl.BlockSpec((None, s, 2 * SLAB), lambda bi, hp: (bi, 0, hp)),
                  pl.BlockSpec((None, nk, pair, ATT_TK), lambda bi, hp: (bi, 0, hp, 0))],
        out_specs=pl.BlockSpec((None, nt, pair, ATT_T), lambda bi, hp: (bi, 0, hp, 0)),
        scratch_shapes=_attn_scratch(HEAD_DIM),
        compiler_params=_params(("parallel", "parallel")),
        name="fox_attn",
    )(q_t, k, v_t)


def _diff_proj_kernel(x_ref, pos_ref, g_ref, wqkv_ref, qg_ref, kg_ref, invf_ref,
                      q_out, k_out, v_out):
    half = ROT_DIM // 2
    h = _rms(x_ref[...], g_ref[...]).astype(BF16)
    ang = invf_ref[...] * pos_ref[...].astype(F32)
    cos = jnp.cos(ang)
    sin = jnp.sin(ang)

    def project(part):
        return lax.dot_general(wqkv_ref[part * D_MODEL:(part + 1) * D_MODEL, :], h, NT_DIMS,
                               preferred_element_type=F32)

    def rope_t(xt):
        x1, x2 = xt[0:half], xt[half:ROT_DIM]
        return jnp.concatenate([x1 * cos - x2 * sin, x2 * cos + x1 * sin, xt[ROT_DIM:]], axis=0)

    def head_t(xt_all, hh, gain_ref):
        maps = [rope_t(_head_norm_t(xt_all[hh * SLAB + mp * HEAD_DIM:hh * SLAB + (mp + 1) * HEAD_DIM, :],
                                    gain_ref[...])) for mp in range(2)]
        return jnp.concatenate(maps, axis=0)

    kt_all = project(1)
    for hh in range(DIFF_HEADS):
        k_out[:, hh * SLAB:(hh + 1) * SLAB] = head_t(kt_all, hh, kg_ref).T.astype(BF16)
    qt_all = project(0)
    for hh in range(DIFF_HEADS):
        q_out[hh] = (head_t(qt_all, hh, qg_ref) * QSCALE).astype(BF16)
    vt_all = project(2)
    for c in range(ATT_T // ATT_TK):
        v_out[c] = vt_all[:, c * ATT_TK:(c + 1) * ATT_TK].astype(BF16)


def _diff_proj(x, pos, gain, wqkv_t, q_gain, k_gain, inv_freq, layer, idx):
    b, s, d = x.shape
    nt = s // ATT_T
    tile = lambda bi, i: (bi, i, 0)
    return pl.pallas_call(
        _diff_proj_kernel,
        out_shape=(jax.ShapeDtypeStruct((b, nt, DIFF_HEADS, SLAB, ATT_T), BF16),
                   jax.ShapeDtypeStruct((b, s, d), BF16),
                   jax.ShapeDtypeStruct((b, s // ATT_TK, d, ATT_TK), BF16)),
        grid=(b, nt),
        in_specs=[pl.BlockSpec((None, ATT_T, d), tile),
                  pl.BlockSpec((None, 1, ATT_T), lambda bi, i: (bi, 0, i)),
                  _layer_weight((1, d))(layer),
                  _layer_weight((3 * d, d))(idx),
                  _layer_weight((HEAD_DIM, 1))(idx),
                  _layer_weight((HEAD_DIM, 1))(idx),
                  _resident((ROT_DIM // 2, 1))],
        out_specs=(pl.BlockSpec((None, None, DIFF_HEADS, SLAB, ATT_T), lambda bi, i: (bi, i, 0, 0, 0)),
                   pl.BlockSpec((None, ATT_T, d), tile),
                   pl.BlockSpec((None, ATT_T // ATT_TK, d, ATT_TK), lambda bi, i: (bi, i, 0, 0))),
        compiler_params=_params(("parallel", "parallel")),
        name="diff_proj",
    )(x, pos, gain, wqkv_t, q_gain, k_gain, inv_freq)


def _diff_attn_kernel(q_ref, k_ref, v_ref, lq1_ref, lk1_ref, lq2_ref, lk2_ref, sg_ref, o_ref,
                      qm_ref, bias_ref, s_ref, smax_ref, m_ref, l_ref, acc_ref, *, lambda_init):
    lam = (jnp.exp(jnp.sum(lq1_ref[...] * lk1_ref[...], axis=1, keepdims=True))
           - jnp.exp(jnp.sum(lq2_ref[...] * lk2_ref[...], axis=1, keepdims=True)) + lambda_init)
    feat = lax.broadcasted_iota(jnp.int32, (SLAB, ATT_T), 0)
    _fill_diag_bias(bias_ref, lambda key, qry: (key // CHUNK) <= (qry // CHUNK))

    def query_tile(qi, carry):
        q = q_ref[qi].astype(F32)
        qm_ref[0] = jnp.where(feat < HEAD_DIM, q, 0.0).astype(BF16)
        qm_ref[1] = jnp.where(feat >= HEAD_DIM, q, 0.0).astype(BF16)
        _attend(lambda j, c0: qm_ref[j, :, c0:],
                lambda ki, j: k_ref[_key_rows(ki), :],
                lambda ki, j: v_ref[ki],
                bias_ref, s_ref, smax_ref, m_ref, l_ref, acc_ref, qi)
        o_t = acc_ref[0] / l_ref[0] - lam * (acc_ref[1] / l_ref[1])
        o_t = o_t * lax.rsqrt(jnp.mean(o_t * o_t, axis=0, keepdims=True) + EPS) * sg_ref[...]
        o_ref[qi] = (o_t * (1.0 - lambda_init)).astype(BF16)
        return carry

    lax.fori_loop(0, q_ref.shape[0], query_tile, 0)


def _diff_attn(q_t, k, v_t, lq1, lk1, lq2, lk2, sub_gain, idx, lambda_init):
    b, nk, d, _ = v_t.shape
    s = nk * ATT_TK
    nt = s // ATT_T
    vec = _layer_weight((1, HEAD_DIM))(idx)
    return pl.pallas_call(
        functools.partial(_diff_attn_kernel, lambda_init=lambda_init),
        out_shape=jax.ShapeDtypeStruct((b, nt, d, ATT_T), BF16),
        grid=(b, DIFF_HEADS),
        in_specs=[pl.BlockSpec((None, nt, None, SLAB, ATT_T), lambda bi, hh: (bi, 0, hh, 0, 0)),
                  pl.BlockSpec((None, s, SLAB), lambda bi, hh: (bi, 0, hh)),
                  pl.BlockSpec((None, nk, SLAB, ATT_TK), lambda bi, hh: (bi, 0, hh, 0)),
                  vec, vec, vec, vec,
                  _layer_weight((2 * HEAD_DIM, 1))(idx)],
        out_specs=pl.BlockSpec((None, nt, SLAB, ATT_T), lambda bi, hh: (bi, 0, hh, 0)),
        scratch_shapes=[pltpu.VMEM((2, SLAB, ATT_T), BF16)] + _attn_scratch(2 * HEAD_DIM),
        compiler_params=_params(("parallel", "parallel")),
        name="diff_attn",
    )(q_t, k, v_t, lq1, lk1, lq2, lk2, sub_gain)


def _conv_kernel(x_ref, g_ref, w1_ref, b1_ref, wdw_ref, bdw_ref, lng_ref, lnb_ref, w2_ref, b2_ref,
                 out_ref, u_ref, us_ref, a_ref):
    tm, d = x_ref.shape

    @pl.when(pl.program_id(1) == 0)
    def _():
        u_ref[0:CONV_HALO, :] = jnp.zeros((CONV_HALO, d), F32)

    @pl.when(pl.program_id(1) > 0)
    def _():
        u_ref[0:CONV_HALO, :] = u_ref[tm:tm + CONV_HALO, :]

    x = x_ref[...]
    h = _rms(x, g_ref[...]).astype(BF16)
    u = jnp.dot(h, w1_ref[...], preferred_element_type=F32) + b1_ref[...]
    u_ref[CONV_HALO:CONV_HALO + tm, :] = u[:, :d] * jax.nn.sigmoid(u[:, d:])
    span = us_ref.shape[1]
    for b in range(1, SUBLANES):
        us_ref[b - 1] = u_ref[b:b + span, :]

    first = CONV_HALO - (CONV_WIDTH - 1)
    for c in range(tm // CONV_RC):
        base = c * CONV_RC
        acc = jnp.broadcast_to(bdw_ref[...], (CONV_RC, d))
        for j in range(CONV_WIDTH):
            b = (first + j) % SUBLANES
            lo = base + first + j - b
            rows = u_ref[lo:lo + CONV_RC, :] if b == 0 else us_ref[b - 1, lo:lo + CONV_RC, :]
            acc = acc + wdw_ref[j:j + 1, :] * rows
        mu = jnp.mean(acc, axis=-1, keepdims=True)
        cen = acc - mu
        var = jnp.mean(cen * cen, axis=-1, keepdims=True)
        y = cen * lax.rsqrt(var + EPS) * lng_ref[...] + lnb_ref[...]
        a_ref[base:base + CONV_RC, :] = (y * jax.nn.sigmoid(y)).astype(BF16)

    out_ref[...] = x + jnp.dot(a_ref[...], w2_ref[...], preferred_element_type=F32) + b2_ref[...]


def _conv_module(x, gain, w1, b1, wdw, bdw, lng, lnb, w2, b2, layer, idx):
    b, s, d = x.shape
    tile = pl.BlockSpec((None, CONV_TM, d), lambda bi, i: (bi, i, 0))
    vec = _layer_weight((1, d))(idx)
    return pl.pallas_call(
        _conv_kernel,
        out_shape=jax.ShapeDtypeStruct((b, s, d), F32),
        grid=(b, s // CONV_TM),
        in_specs=[tile,
                  _layer_weight((1, d))(layer),
                  _layer_weight((d, 2 * d))(idx),
                  _layer_weight((1, 2 * d))(idx),
                  _layer_weight((CONV_WIDTH, d))(idx),
                  vec, vec, vec,
                  _layer_weight((d, d))(idx),
                  vec],
        out_specs=tile,
        scratch_shapes=[pltpu.VMEM((CONV_HALO + CONV_TM, d), F32),
                        pltpu.VMEM((SUBLANES - 1, CONV_HALO + CONV_TM - SUBLANES, d), F32),
                        pltpu.VMEM((CONV_TM, d), BF16)],
        compiler_params=_params(("parallel", "arbitrary")),
        name="conv_module",
    )(x, gain, w1, b1, wdw, bdw, lng, lnb, w2, b2)


def kernel(x, positions, norm_ffn1, norm_mix, norm_ffn2, ffn1_w_gate, ffn1_w_up, ffn1_w_down, ffn2_w_gate, ffn2_w_up, ffn2_w_down, fox_w_in, fox_b_f, fox_q_gain, fox_k_gain, fox_w_out, conv_w_pw1, conv_b_pw1, conv_w_dw, conv_b_dw, conv_ln_g, conv_ln_b, conv_w_pw2, conv_b_pw2, diff_w_in, diff_q_gain, diff_k_gain, diff_lambda_q1, diff_lambda_k1, diff_lambda_q2, diff_lambda_k2, diff_sub_gain, diff_w_out):
    b, s, d = x.shape
    depth = norm_ffn1.shape[0]
    assert d == D_MODEL and s % ATT_T == 0 and s % CONV_TM == 0 and (b * s) % FFN_TM == 0

    bf = lambda w: w.astype(BF16)
    row = lambda v: v[:, None, :]
    col = lambda v: v[:, :, None]
    g1, gm, g2 = row(norm_ffn1), row(norm_mix), row(norm_ffn2)
    f1 = (bf(ffn1_w_gate), bf(ffn1_w_up), bf(ffn1_w_down))
    f2 = (bf(ffn2_w_gate), bf(ffn2_w_up), bf(ffn2_w_down))
    fox_wqkv_t = bf(jnp.swapaxes(fox_w_in[:, :, :3 * d], 1, 2))
    fox_wf_t = bf(jnp.swapaxes(fox_w_in[:, :, 3 * d:], 1, 2))
    fox_wo = bf(fox_w_out)
    diff_wqkv_t = bf(jnp.swapaxes(diff_w_in, 1, 2))
    diff_wo = bf(diff_w_out)
    conv_w1, conv_w2 = bf(conv_w_pw1), bf(conv_w_pw2)
    tri = (jnp.arange(ATT_T)[:, None] <= jnp.arange(ATT_T)[None, :]).astype(BF16)
    inv_freq = (ROPE_THETA ** (-jnp.arange(0, ROT_DIM, 2, dtype=F32) / ROT_DIM))[:, None]
    pos = positions[:, None, :]

    flat = lambda a: a.reshape(b * s, d)
    i_fox = i_conv = i_diff = 0
    x2 = flat(x)
    for i in range(depth):
        x2 = _ffn(x2, g1, *f1, i)
        x3 = x2.reshape(b, s, d)
        kind = i % N_MIXERS
        if kind == 0:
            q_t, k, v_t = _fox_proj(x3, gm, fox_wqkv_t, fox_wf_t, col(fox_b_f),
                                    col(fox_q_gain), col(fox_k_gain), tri, i, i_fox)
            o = _fox_attn(q_t, k, v_t)
            x2 = _ffn(x2, g2, *f2, i, pre=(o, fox_wo, i_fox))
            i_fox += 1
        elif kind == 1:
            x3 = _conv_module(x3, gm, conv_w1, row(conv_b_pw1), conv_w_dw, row(conv_b_dw),
                              row(conv_ln_g), row(conv_ln_b), conv_w2, row(conv_b_pw2), i, i_conv)
            x2 = _ffn(flat(x3), g2, *f2, i)
            i_conv += 1
        else:
            lambda_init = 0.8 - 0.6 * math.exp(-0.3 * i)
            q_t, k, v_t = _diff_proj(x3, pos, gm, diff_wqkv_t, col(diff_q_gain),
                                     col(diff_k_gain), inv_freq, i, i_diff)
            o = _diff_attn(q_t, k, v_t, row(diff_lambda_q1), row(diff_lambda_k1),
                           row(diff_lambda_q2), row(diff_lambda_k2), col(diff_sub_gain),
                           i_diff, lambda_init)
            x2 = _ffn(x2, g2, *f2, i, pre=(o, diff_wo, i_diff))
            i_diff += 1
    return x2.reshape(b, s, d)
```
